```python
import jax, jax.numpy as jnp
from jax import lax
import numpy as np

D_MODEL = 1024
BATCH = 8
SEQ = 8192
DEPTH = 1
DEC_BATCH = 8
DEC_SEQ = 16
PAST_LEN = 1024

CHUNK = 64
SB_BLOCK = 128
SB_HEADS = 8
SB_HEAD_DIM = 64
RET_HEADS = 4
RET_QK_DIM = 64
RET_V_DIM = 128
D_SB = SB_HEADS * SB_HEAD_DIM
D_RET_QK = RET_HEADS * RET_QK_DIM
D_RET_V = RET_HEADS * RET_V_DIM
D_MIX = D_SB + D_RET_V
D_IN = 3 * D_SB + 2 * D_RET_QK + 2 * D_RET_V
D_FF = -(-8 * D_MODEL // (3 * 256)) * 256
D_PLE = 256
ROPE_BASE = 10000.0
EPS = 1e-6

kernel_name = "hymba_stickbreak_retnet_stream"


def rmsnorm(x, g):
    xf = x.astype(jnp.float32)
    y = xf * lax.rsqrt(jnp.mean(xf * xf, axis=-1, keepdims=True) + EPS)
    return y.astype(x.dtype) * g


def ret_log_gamma():
    return jnp.log(1.0 - 2.0 ** (-5.0 - jnp.arange(RET_HEADS, dtype=jnp.float32)))


def rotary(x, pos):
    half = x.shape[-1] // 2
    inv = ROPE_BASE ** (-jnp.arange(half, dtype=jnp.float32) / half)
    ang = pos.astype(jnp.float32)[:, None] * inv[None, :]
    cos, sin = jnp.cos(ang), jnp.sin(ang)
    x1, x2 = x[..., :half], x[..., half:]
    return jnp.concatenate([x1 * cos - x2 * sin, x1 * sin + x2 * cos], axis=-1)


def split_heads(x, h, d):
    b, t, _ = x.shape
    return x.reshape(b, t, h, d).transpose(0, 2, 1, 3)


def merge_heads(x):
    b, h, t, d = x.shape
    return x.transpose(0, 2, 1, 3).reshape(b, t, h * d)


def stick_breaking(q, k, v, q_pos, k_pos):
    z = jnp.einsum('bhqd,bhkd->bhqk', q, k).astype(jnp.float32) * (q.shape[-1] ** -0.5)
    causal = k_pos[None, :] < q_pos[:, None]
    log_fail = jnp.where(causal, -jax.nn.softplus(z), 0.0)
    suffix = lax.cumsum(log_fail, axis=log_fail.ndim - 1, reverse=True) - log_fail
    a = jnp.where(causal, jnp.exp(jax.nn.log_sigmoid(z) + suffix), 0.0)
    return jnp.einsum('bhqk,bhkd->bhqd', a.astype(v.dtype), v)


def stick_breaking_prompt(q, k, v):
    t = q.shape[2]
    outs = []
    for blk in range(t // SB_BLOCK):
        s0, e = blk * SB_BLOCK, (blk + 1) * SB_BLOCK
        outs.append(stick_breaking(q[:, :, s0:e], k[:, :, :e], v[:, :, :e],
                                   jnp.arange(s0, e), jnp.arange(e)))
    return jnp.concatenate(outs, axis=2)


def retention_step(state, q, k, v, lg):
    L = q.shape[2]
    idx = jnp.arange(L, dtype=jnp.float32)
    rel = idx[:, None] - idx[None, :]
    decay = jnp.where(rel >= 0, jnp.exp(lg[:, None, None] * jnp.maximum(rel, 0.0)), 0.0)
    scores = jnp.einsum('bhid,bhjd->bhij', q, k) * decay
    intra = jnp.einsum('bhij,bhje->bhie', scores, v)
    q_dec = q * jnp.exp(lg[:, None] * (idx[None, :] + 1.0))[..., None]
    cross = jnp.einsum('bhid,bhde->bhie', q_dec, state)
    k_dec = k * jnp.exp(lg[:, None] * (L - 1.0 - idx[None, :]))[..., None]
    new_state = jnp.exp(lg * L)[:, None, None] * state + jnp.einsum('bhjd,bhje->bhde', k_dec, v)
    return new_state, intra + cross


def retention_prompt(q, k, v, lg):
    b, h, t, dk = q.shape
    n = t // CHUNK
    to_chunks = lambda a: jnp.moveaxis(a.reshape(b, h, n, CHUNK, a.shape[-1]), 2, 0)
    s0 = jnp.zeros((b, h, dk, v.shape[-1]), jnp.float32)
    s_fin, outs = lax.scan(lambda s, c: retention_step(s, c[0], c[1], c[2], lg), s0,
                           (to_chunks(q), to_chunks(k), to_chunks(v)))
    out = jnp.moveaxis(outs, 0, 2).reshape(b, h, t, v.shape[-1])
    return s_fin, out


def mixer_inputs(xn, w_in, pos):
    proj = xn @ w_in
    c1 = D_SB; c2 = 2 * D_SB; c3 = 3 * D_SB
    c4 = c3 + D_RET_QK; c5 = c4 + D_RET_QK; c6 = c5 + D_RET_V
    q_sb, k_sb, v_sb, q_r, k_r, v_r, g_r = jnp.split(proj, [c1, c2, c3, c4, c5, c6], axis=-1)
    q_sb = split_heads(q_sb, SB_HEADS, SB_HEAD_DIM)
    k_sb = split_heads(k_sb, SB_HEADS, SB_HEAD_DIM)
    v_sb = split_heads(v_sb, SB_HEADS, SB_HEAD_DIM)
    q_r = rotary(split_heads(q_r, RET_HEADS, RET_QK_DIM).astype(jnp.float32), pos)
    k_r = rotary(split_heads(k_r, RET_HEADS, RET_QK_DIM).astype(jnp.float32), pos) * (RET_QK_DIM ** -0.5)
    v_r = split_heads(v_r, RET_HEADS, RET_V_DIM).astype(jnp.float32)
    return q_sb, k_sb, v_sb, q_r, k_r, v_r, g_r


def mixer_output(o_sb, o_r, g_r, ret_gn, w_out, dtype):
    o_r = o_r * lax.rsqrt(jnp.mean(o_r * o_r, axis=-1, keepdims=True) + EPS)
    o_r = merge_heads(o_r).astype(dtype) * ret_gn * jax.nn.silu(g_r)
    cat = jnp.concatenate([merge_heads(o_sb).astype(dtype), o_r], axis=-1)
    return cat @ w_out


def channel_and_ple(h, p, g_ffn, w_ffn_gate, w_ffn_up, w_ffn_down, g_ple, w_ple_gate, w_ple):
    hn = rmsnorm(h, g_ffn)
    h = h + (jax.nn.silu(hn @ w_ffn_gate) * (hn @ w_ffn_up)) @ w_ffn_down
    gate = jax.nn.sigmoid(rmsnorm(h, g_ple) @ w_ple_gate)
    return h + (p @ w_ple) * gate


def setup_inputs(seed: int = 0) -> dict:
    key = jax.random.key(seed)
    ks = jax.random.split(key, 24)
    nrm = lambda k, shape, s: jax.random.normal(k, shape, jnp.float32) * s
    gain = lambda k, shape: 1.0 + 0.01 * jax.random.normal(k, shape, jnp.float32)
    return {
        "x_prompt": nrm(ks[0], (BATCH, SEQ, D_MODEL), 1.0),
        "x_sample": nrm(ks[1], (DEC_BATCH, DEC_SEQ, D_MODEL), 1.0),
        "cache_sb_k": nrm(ks[2], (DEPTH, DEC_BATCH, SB_HEADS, PAST_LEN, SB_HEAD_DIM), 1.0),
        "cache_sb_v": nrm(ks[3], (DEPTH, DEC_BATCH, SB_HEADS, PAST_LEN, SB_HEAD_DIM), 1.0),
        "state_ret": nrm(ks[4], (DEPTH, DEC_BATCH, RET_HEADS, RET_QK_DIM, RET_V_DIM), 0.5),
        "p_prompt": nrm(ks[5], (DEPTH, BATCH, SEQ, D_PLE), 1.0),
        "p_sample": nrm(ks[6], (DEPTH, DEC_BATCH, DEC_SEQ, D_PLE), 1.0),
        "g_mix": gain(ks[7], (DEPTH, D_MODEL)),
        "w_in": nrm(ks[8], (DEPTH, D_MODEL, D_IN), D_MODEL ** -0.5),
        "ret_gn": gain(ks[9], (DEPTH, D_RET_V)),
        "w_out": nrm(ks[10], (DEPTH, D_MIX, D_MODEL), D_MIX ** -0.5),
        "g_ffn": gain(ks[11], (DEPTH, D_MODEL)),
        "w_ffn_gate": nrm(ks[12], (DEPTH, D_MODEL, D_FF), D_MODEL ** -0.5),
        "w_ffn_up": nrm(ks[13], (DEPTH, D_MODEL, D_FF), D_MODEL ** -0.5),
        "w_ffn_down": nrm(ks[14], (DEPTH, D_FF, D_MODEL), D_FF ** -0.5),
        "g_ple": gain(ks[15], (DEPTH, D_MODEL)),
        "w_ple_gate": nrm(ks[16], (DEPTH, D_MODEL, D_MODEL), D_MODEL ** -0.5),
        "w_ple": nrm(ks[17], (DEPTH, D_PLE, D_MODEL), D_PLE ** -0.5),
        "g_final": gain(ks[18], (D_MODEL,)),
    }


def reference(x_prompt, x_sample, cache_sb_k, cache_sb_v, state_ret, p_prompt, p_sample,
              g_mix, w_in, ret_gn, w_out, g_ffn, w_ffn_gate, w_ffn_up, w_ffn_down,
              g_ple, w_ple_gate, w_ple, g_final):
    t_p = x_prompt.shape[1]
    t_s = x_sample.shape[1]
    past = cache_sb_k.shape[3]
    pos_p = jnp.arange(t_p)
    pos_s = past + jnp.arange(t_s)
    lg = ret_log_gamma()
    hp, hs = x_prompt, x_sample
    nk_p, nv_p, ns_p, nk_s, nv_s, ns_s = [], [], [], [], [], []
    for i in range(DEPTH):
        xn = rmsnorm(hp, g_mix[i])
        q_sb, k_sb, v_sb, q_r, k_r, v_r, g_r = mixer_inputs(xn, w_in[i], pos_p)
        o_sb = stick_breaking_prompt(q_sb, k_sb, v_sb)
        s_p, o_r = retention_prompt(q_r, k_r, v_r, lg)
        hp = hp + mixer_output(o_sb, o_r, g_r, ret_gn[i], w_out[i], hp.dtype)
        hp = channel_and_ple(hp, p_prompt[i], g_ffn[i], w_ffn_gate[i], w_ffn_up[i], w_ffn_down[i],
                             g_ple[i], w_ple_gate[i], w_ple[i])
        nk_p.append(k_sb); nv_p.append(v_sb); ns_p.append(s_p.astype(x_prompt.dtype))
        xn = rmsnorm(hs, g_mix[i])
        q_sb, k_sb, v_sb, q_r, k_r, v_r, g_r = mixer_inputs(xn, w_in[i], pos_s)
        k_all = jnp.concatenate([cache_sb_k[i].astype(k_sb.dtype), k_sb], axis=2)
        v_all = jnp.concatenate([cache_sb_v[i].astype(v_sb.dtype), v_sb], axis=2)
        o_sb = stick_breaking(q_sb, k_all, v_all, pos_s, jnp.arange(past + t_s))
        s_s, o_r = retention_step(state_ret[i].astype(jnp.float32), q_r, k_r, v_r, lg)
        hs = hs + mixer_output(o_sb, o_r, g_r, ret_gn[i], w_out[i], hs.dtype)
        hs = channel_and_ple(hs, p_sample[i], g_ffn[i], w_ffn_gate[i], w_ffn_up[i], w_ffn_down[i],
                             g_ple[i], w_ple_gate[i], w_ple[i])
        nk_s.append(k_sb); nv_s.append(v_sb); ns_s.append(s_s.astype(state_ret.dtype))
    y_prompt = rmsnorm(hp, g_final)
    y_sample = rmsnorm(hs, g_final)
    return (y_prompt, y_sample, jnp.stack(nk_p), jnp.stack(nv_p), jnp.stack(ns_p),
            jnp.stack(nk_s), jnp.stack(nv_s), jnp.stack(ns_s))
```

```python
import functools

import jax
import jax.numpy as jnp
from jax import lax
from jax.experimental import pallas as pl
from jax.experimental.pallas import tpu as pltpu

F32 = jnp.float32
BF16 = jnp.bfloat16

D_MODEL = 1024
SB_HEADS = 8
SB_HEAD_DIM = 64
RET_HEADS = 4
RET_QK_DIM = 64
RET_V_DIM = 128
D_SB = SB_HEADS * SB_HEAD_DIM
D_RET_QK = RET_HEADS * RET_QK_DIM
D_RET_V = RET_HEADS * RET_V_DIM
D_IN = 3 * D_SB + 2 * D_RET_QK + 2 * D_RET_V
D_FF = 2816
D_PLE = 256
ROPE_BASE = 10000.0
EPS = 1e-6

LANES = 128
VMEM_LIMIT = 56 * 1024 * 1024

PROJ_CHUNK = 512
FF_CHUNK = 256
SB_TQ = 256
SB_TK = 256
RET_CHUNK = 256


def _rms(x):
    return x * lax.rsqrt(jnp.mean(x * x, axis=-1, keepdims=True) + EPS)


def _inproj_kernel(x_ref, g_ref, w_ref, sb_ref, kv_ref, ret_ref):
    xb = (_rms(x_ref[...]) * g_ref[...]).astype(BF16)
    for c in range(0, D_IN, PROJ_CHUNK):
        acc = jnp.dot(xb, w_ref[:, c:c + PROJ_CHUNK], preferred_element_type=F32)
        if c < D_SB:
            sb_ref[:, c:c + PROJ_CHUNK] = (acc * (SB_HEAD_DIM ** -0.5)).astype(BF16)
        elif c < 3 * D_SB:
            sb_ref[:, c:c + PROJ_CHUNK] = acc.astype(BF16)
            kv_ref[:, c - D_SB:c - D_SB + PROJ_CHUNK] = acc
        else:
            ret_ref[:, c - 3 * D_SB:c - 3 * D_SB + PROJ_CHUNK] = acc


def _inproj(x2, g, w_bf, tm):
    n = x2.shape[0]
    return pl.pallas_call(
        _inproj_kernel,
        grid=(n // tm,),
        in_specs=[
            pl.BlockSpec((tm, D_MODEL), lambda i: (i, 0)),
            pl.BlockSpec((1, D_MODEL), lambda i: (0, 0)),
            pl.BlockSpec((D_MODEL, D_IN), lambda i: (0, 0)),
        ],
        out_specs=[
            pl.BlockSpec((tm, 3 * D_SB), lambda i: (i, 0)),
            pl.BlockSpec((tm, 2 * D_SB), lambda i: (i, 0)),
            pl.BlockSpec((tm, D_IN - 3 * D_SB), lambda i: (i, 0)),
        ],
        out_shape=[
            jax.ShapeDtypeStruct((n, 3 * D_SB), BF16),
            jax.ShapeDtypeStruct((n, 2 * D_SB), F32),
            jax.ShapeDtypeStruct((n, D_IN - 3 * D_SB), F32),
        ],
        compiler_params=pltpu.CompilerParams(
            dimension_semantics=("parallel",), vmem_limit_bytes=VMEM_LIMIT),
        name="inproj",
    )(x2, g, w_bf)


def _sb_kernel(q_ref, k_ref, v_ref, o_ref, *, tq, tk, q_off, n_masked):
    q_start = q_off + pl.program_id(2) * tq
    n_free = q_start // tk
    lane = lax.broadcasted_iota(jnp.int32, (1, LANES), 1)
    head_masks = [(lane // SB_HEAD_DIM) == h for h in range(2)]
    q = q_ref[0]
    q_heads = [jnp.where(m, q, jnp.zeros_like(q)) for m in head_masks]
    row_pos = q_start + lax.broadcasted_iota(jnp.int32, (tq, 1), 0)
    jj = lax.broadcasted_iota(jnp.int32, (tk, tk), 0)
    ss = lax.broadcasted_iota(jnp.int32, (tk, tk), 1)
    tri = jnp.where(jj >= ss, 1.0, 0.0).astype(BF16)

    def sweep(kb, state, masked):
        carries, acc = state
        k = k_ref[0, pl.ds(pl.multiple_of(kb * tk, tk), tk), :]
        v = v_ref[0, pl.ds(pl.multiple_of(kb * tk, tk), tk), :]
        if masked:
            col_pos = kb * tk + lax.broadcasted_iota(jnp.int32, (1, tk), 1)
            causal = col_pos < row_pos
        new_carries = []
        for h in range(2):
            z = lax.dot_general(q_heads[h], k, (((1,), (1,)), ((), ())),
                                preferred_element_type=F32)
            log_fail = -(jnp.maximum(z, 0.0) + jnp.log(1.0 + jnp.exp(-jnp.abs(z))))
            if masked:
                log_fail = jnp.where(causal, log_fail, 0.0)
            hi = log_fail.astype(BF16)
            lo = (log_fail - hi.astype(F32)).astype(BF16)
            incl = (jnp.dot(hi, tri, preferred_element_type=F32)
                    + jnp.dot(lo, tri, preferred_element_type=F32))
            a = jnp.exp(z + incl + carries[h])
            if masked:
                a = jnp.where(causal, a, 0.0)
            vh = jnp.where(head_masks[h], v, jnp.zeros_like(v))
            acc = acc + jnp.dot(a.astype(BF16), vh, preferred_element_type=F32)
            new_carries.append(carries[h] + jnp.sum(log_fail, axis=-1, keepdims=True))
        return tuple(new_carries), acc

    state = ((jnp.zeros((tq, 1), F32), jnp.zeros((tq, 1), F32)), jnp.zeros((tq, LANES), F32))
    for i in range(n_masked):
        state = sweep(n_free + (n_masked - 1 - i), state, True)
    state = lax.fori_loop(0, n_free, lambda i, s: sweep(n_free - 1 - i, s, False), state)
    o_ref[0] = state[1].astype(o_ref.dtype)


def _sb_attention(q_arr, k_arr, v_arr, q_col, k_col, v_col, tq, tk, q_off):
    b, t_q, _ = q_arr.shape
    t_k = k_arr.shape[1]
    n_q = t_q // tq
    n_masked = -(-(q_off % tk + tq) // tk)
    assert t_k % tk == 0 and (q_off + t_q + tk - 1) // tk <= t_k // tk
    kern = functools.partial(_sb_kernel, tq=tq, tk=tk, q_off=q_off, n_masked=n_masked)
    return pl.pallas_call(
        kern,
        grid=(b, D_SB // LANES, n_q),
        in_specs=[
            pl.BlockSpec((1, tq, LANES), lambda bi, hp, qi: (bi, qi, q_col + hp)),
            pl.BlockSpec((1, t_k, LANES), lambda bi, hp, qi: (bi, 0, k_col + hp)),
            pl.BlockSpec((1, t_k, LANES), lambda bi, hp, qi: (bi, 0, v_col + hp)),
        ],
        out_specs=pl.BlockSpec((1, tq, LANES), lambda bi, hp, qi: (bi, qi, hp)),
        out_shape=jax.ShapeDtypeStruct((b, t_q, D_SB), BF16),
        compiler_params=pltpu.CompilerParams(
            dimension_semantics=("parallel", "parallel", "arbitrary"),
            vmem_limit_bytes=VMEM_LIMIT),
        name="stick_breaking",
    )(q_arr, k_arr, v_arr)


def _ret_kernel(qk_ref, v_ref, g_ref, cos_ref, sin_ref, dmat_ref, qdec_ref, kdec_ref, cdec_ref,
                gn_ref, s0_ref, o_ref, sfin_ref, state_ref, *, chunk):
    c = pl.program_id(1)

    @pl.when(c == 0)
    def _():
        state_ref[...] = s0_ref[0]

    lane = lax.broadcasted_iota(jnp.int32, (1, LANES), 1)
    first_half = (lane % RET_QK_DIM) < (RET_QK_DIM // 2)
    head_masks = [(lane // RET_QK_DIM) == h for h in range(2)]
    cos = cos_ref[...]
    sin = sin_ref[...]

    def rotary(x):
        swapped = jnp.where(first_half,
                            pltpu.roll(x, LANES - RET_QK_DIM // 2, 1),
                            pltpu.roll(x, RET_QK_DIM // 2, 1))
        return x * cos + swapped * sin

    for p in range(RET_HEADS // 2):
        q = rotary(qk_ref[0, :, p * LANES:(p + 1) * LANES])
        k = rotary(qk_ref[0, :, D_RET_QK + p * LANES:D_RET_QK + (p + 1) * LANES]) * (RET_QK_DIM ** -0.5)
        q_b = q.astype(BF16)
        k_b = k.astype(BF16)
        q_dec = (q * qdec_ref[p]).astype(BF16)
        k_dec = (k * kdec_ref[p]).astype(BF16)
        rows = slice(p * LANES, (p + 1) * LANES)
        state = state_ref[rows, :]
        state_b = state.astype(BF16)
        new_state = state * cdec_ref[rows, :]
        for hh in range(2):
            h = 2 * p + hh
            cols = slice(h * RET_V_DIM, (h + 1) * RET_V_DIM)
            v_b = v_ref[0, :, cols].astype(BF16)
            qh = jnp.where(head_masks[hh], q_b, jnp.zeros_like(q_b))
            scores = lax.dot_general(qh, k_b, (((1,), (1,)), ((), ())),
                                     preferred_element_type=F32) * dmat_ref[h]
            intra = jnp.dot(scores.astype(BF16), v_b, preferred_element_type=F32)
            qdh = jnp.where(head_masks[hh], q_dec, jnp.zeros_like(q_dec))
            cross = jnp.dot(qdh, state_b, preferred_element_type=F32)
            kdh = jnp.where(head_masks[hh], k_dec, jnp.zeros_like(k_dec))
            new_state = new_state + lax.dot_general(kdh, v_b, (((0,), (0,)), ((), ())),
                                                    preferred_element_type=F32)
            o = _rms(intra + cross)
            gate = g_ref[0, :, cols]
            o_ref[0, :, cols] = (o * gn_ref[:, cols] * (gate * jax.nn.sigmoid(gate))).astype(o_ref.dtype)
        state_ref[rows, :] = new_state

    @pl.when(c == pl.num_programs(1) - 1)
    def _():
        sfin_ref[0] = state_ref[...]


def _ret_tables(chunk, pos0, t):
    half = RET_QK_DIM // 2
    inv = ROPE_BASE ** (-jnp.arange(half, dtype=F32) / half)
    ang = (pos0 + jnp.arange(t)).astype(F32)[:, None] * inv[None, :]
    cos, sin = jnp.cos(ang), jnp.sin(ang)
    cos_t = jnp.tile(cos, (1, LANES // half))
    sin_t = jnp.tile(jnp.concatenate([-sin, sin], axis=-1), (1, LANES // RET_QK_DIM))
    lg = jnp.log(1.0 - 2.0 ** (-5.0 - jnp.arange(RET_HEADS, dtype=F32)))
    idx = jnp.arange(chunk, dtype=F32)
    rel = idx[:, None] - idx[None, :]
    dmat = jnp.where(rel >= 0, jnp.exp(lg[:, None, None] * jnp.maximum(rel, 0.0)), 0.0)
    qdec = jnp.exp(lg[:, None] * (idx[None, :] + 1.0))
    kdec = jnp.exp(lg[:, None] * (chunk - 1.0 - idx[None, :]))
    expand = lambda a: jnp.repeat(a.reshape(RET_HEADS // 2, 2, chunk), RET_QK_DIM, axis=1).transpose(0, 2, 1)
    cdec = jnp.repeat(jnp.exp(lg * chunk), RET_QK_DIM)[:, None] * jnp.ones((1, RET_V_DIM), F32)
    return cos_t, sin_t, dmat, expand(qdec), expand(kdec), cdec


def _retention(ret_arr, ret_gn, state0, chunk, pos0):
    b, t, _ = ret_arr.shape
    n_c = t // chunk
    cos_t, sin_t, dmat, qdec, kdec, cdec = _ret_tables(chunk, pos0, t)
    kern = functools.partial(_ret_kernel, chunk=chunk)
    const = lambda *shape: pl.BlockSpec(shape, lambda bi, ci: (0,) * len(shape))
    return pl.pallas_call(
        kern,
        grid=(b, n_c),
        in_specs=[
            pl.BlockSpec((1, chunk, 2 * D_RET_QK), lambda bi, ci: (bi, ci, 0)),
            pl.BlockSpec((1, chunk, D_RET_V), lambda bi, ci: (bi, ci, 1)),
            pl.BlockSpec((1, chunk, D_RET_V), lambda bi, ci: (bi, ci, 2)),
            pl.BlockSpec((chunk, LANES), lambda bi, ci: (ci, 0)),
            pl.BlockSpec((chunk, LANES), lambda bi, ci: (ci, 0)),
            const(RET_HEADS, chunk, chunk),
            const(RET_HEADS // 2, chunk, LANES),
            const(RET_HEADS // 2, chunk, LANES),
            const(RET_HEADS * RET_QK_DIM, RET_V_DIM),
            const(1, D_RET_V),
            pl.BlockSpec((1, RET_HEADS * RET_QK_DIM, RET_V_DIM), lambda bi, ci: (bi, 0, 0)),
        ],
        out_specs=[
            pl.BlockSpec((1, chunk, D_RET_V), lambda bi, ci: (bi, ci, 0)),
            pl.BlockSpec((1, RET_HEADS * RET_QK_DIM, RET_V_DIM), lambda bi, ci: (bi, 0, 0)),
        ],
        out_shape=[
            jax.ShapeDtypeStruct((b, t, D_RET_V), BF16),
            jax.ShapeDtypeStruct((b, RET_HEADS * RET_QK_DIM, RET_V_DIM), F32),
        ],
        scratch_shapes=[pltpu.VMEM((RET_HEADS * RET_QK_DIM, RET_V_DIM), F32)],
        compiler_params=pltpu.CompilerParams(
            dimension_semantics=("parallel", "arbitrary"), vmem_limit_bytes=VMEM_LIMIT),
        name="retention",
    )(ret_arr, ret_arr, ret_arr, cos_t, sin_t, dmat, qdec, kdec, cdec, ret_gn, state0)


def _mlp_kernel(x_ref, osb_ref, oret_ref, p_ref, w_out_ref, g_ffn_ref, w_gate_ref, w_up_ref, w_down_ref,
                g_ple_ref, w_pg_ref, w_ple_ref, g_final_ref, y_ref):
    h = (x_ref[...]
         + jnp.dot(osb_ref[...], w_out_ref[:D_SB, :], preferred_element_type=F32)
         + jnp.dot(oret_ref[...], w_out_ref[D_SB:, :], preferred_element_type=F32))
    hn = (_rms(h) * g_ffn_ref[...]).astype(BF16)
    ffn = jnp.zeros_like(h)
    for c in range(0, D_FF, FF_CHUNK):
        gate = jnp.dot(hn, w_gate_ref[:, c:c + FF_CHUNK], preferred_element_type=F32)
        up = jnp.dot(hn, w_up_ref[:, c:c + FF_CHUNK], preferred_element_type=F32)
        act = (gate * jax.nn.sigmoid(gate) * up).astype(BF16)
        ffn = ffn + jnp.dot(act, w_down_ref[c:c + FF_CHUNK, :], preferred_element_type=F32)
    h = h + ffn
    hn = (_rms(h) * g_ple_ref[...]).astype(BF16)
    gate = jax.nn.sigmoid(jnp.dot(hn, w_pg_ref[...], preferred_element_type=F32))
    ple = jnp.dot(p_ref[...].astype(BF16), w_ple_ref[...], preferred_element_type=F32)
    h = h + ple * gate
    y_ref[...] = _rms(h) * g_final_ref[...]


def _mlp(x2, osb2, oret2, p2, w_out, g_ffn, w_gate, w_up, w_down, g_ple, w_pg, w_ple, g_final, tm):
    n = x2.shape[0]
    tok = lambda width: pl.BlockSpec((tm, width), lambda i: (i, 0))
    const = lambda a: pl.BlockSpec(a.shape, lambda i: (0, 0), pipeline_mode=pl.Buffered(1))
    weights = (w_out, g_ffn, w_gate, w_up, w_down, g_ple, w_pg, w_ple, g_final)
    return pl.pallas_call(
        _mlp_kernel,
        grid=(n // tm,),
        in_specs=[tok(D_MODEL), tok(D_SB), tok(D_RET_V), tok(D_PLE)] + [const(a) for a in weights],
        out_specs=tok(D_MODEL),
        out_shape=jax.ShapeDtypeStruct((n, D_MODEL), F32),
        compiler_params=pltpu.CompilerParams(
            dimension_semantics=("parallel",), vmem_limit_bytes=VMEM_LIMIT),
        name="token_mlp",
    )(x2, osb2, oret2, p2, *weights)


def _split_heads(a, b, t):
    return a.reshape(b, t, SB_HEADS, SB_HEAD_DIM).transpose(0, 2, 1, 3)


def _merge_heads_bf16(a):
    b, h, t, d = a.shape
    return a.transpose(0, 2, 1, 3).reshape(b, t, h * d).astype(BF16)


def _stream(x, p, w, cache_k, cache_v, state0, tm, tq, chunk):
    b, t, _ = x.shape
    n = b * t
    x2 = x.reshape(n, D_MODEL)
    sb, kv, ret = _inproj(x2, w["g_mix"], w["w_in"], tm)
    sb3 = sb.reshape(b, t, 3 * D_SB)
    new_k = _split_heads(kv[:, :D_SB], b, t)
    new_v = _split_heads(kv[:, D_SB:], b, t)
    n_col = D_SB // LANES
    if cache_k is None:
        o_sb = _sb_attention(sb3, sb3, sb3, 0, n_col, 2 * n_col, tq, SB_TK, 0)
        pos0 = 0
    else:
        past = cache_k.shape[2]
        t_all = -(-(past + t) // SB_TK) * SB_TK
        pad = jnp.zeros((b, t_all - past - t, D_SB), BF16)
        k_all = jnp.concatenate([_merge_heads_bf16(cache_k), sb3[:, :, D_SB:2 * D_SB], pad], axis=1)
        v_all = jnp.concatenate([_merge_heads_bf16(cache_v), sb3[:, :, 2 * D_SB:], pad], axis=1)
        o_sb = _sb_attention(sb3, k_all, v_all, 0, 0, 0, tq, SB_TK, past)
        pos0 = past
    o_ret, new_state = _retention(ret.reshape(b, t, -1), w["ret_gn"], state0, chunk, pos0)
    y = _mlp(x2, o_sb.reshape(n, D_SB), o_ret.reshape(n, D_RET_V), p.reshape(n, D_PLE),
             w["w_out"], w["g_ffn"], w["w_ffn_gate"], w["w_ffn_up"], w["w_ffn_down"],
             w["g_ple"], w["w_ple_gate"], w["w_ple"], w["g_final"], tm)
    new_state = new_state.reshape(b, RET_HEADS, RET_QK_DIM, RET_V_DIM)
    return y.reshape(b, t, D_MODEL), new_k[None], new_v[None], new_state[None]


def kernel(x_prompt, x_sample, cache_sb_k, cache_sb_v, state_ret, p_prompt, p_sample, g_mix, w_in, ret_gn, w_out,
           g_ffn, w_ffn_gate, w_ffn_up, w_ffn_down, g_ple, w_ple_gate, w_ple, g_final):
    assert g_mix.shape[0] == 1, "single-layer model"
    w = {
        "g_mix": g_mix, "w_in": w_in[0].astype(BF16), "ret_gn": ret_gn,
        "w_out": w_out[0].astype(BF16), "g_ffn": g_ffn,
        "w_ffn_gate": w_ffn_gate[0].astype(BF16), "w_ffn_up": w_ffn_up[0].astype(BF16),
        "w_ffn_down": w_ffn_down[0].astype(BF16), "g_ple": g_ple,
        "w_ple_gate": w_ple_gate[0].astype(BF16), "w_ple": w_ple[0].astype(BF16),
        "g_final": g_final[None, :],
    }
    b_p = x_prompt.shape[0]
    b_s, t_s, _ = x_sample.shape
    zero_state = jnp.zeros((b_p, RET_HEADS * RET_QK_DIM, RET_V_DIM), F32)
    y_p, nk_p, nv_p, ns_p = _stream(x_prompt, p_prompt[0], w, None, None, zero_state,
                                    tm=256, tq=SB_TQ, chunk=RET_CHUNK)
    y_s, nk_s, nv_s, ns_s = _stream(x_sample, p_sample[0], w, cache_sb_k[0], cache_sb_v[0],
                                    state_ret[0].reshape(b_s, RET_HEADS * RET_QK_DIM, RET_V_DIM),
                                    tm=b_s * t_s, tq=t_s, chunk=t_s)
    return y_p, y_s, nk_p, nv_p, ns_p, nk_s, nv_s, ns_s
```

```python
import functools
import math

import jax
import jax.numpy as jnp
from jax import lax
from jax.experimental import pallas as pl
from jax.experimental.pallas import tpu as pltpu

F32 = jnp.float32
BF16 = jnp.bfloat16

D_MODEL = 1024
SB_HEADS = 8
SB_HEAD_DIM = 64
RET_HEADS = 4
RET_QK_DIM = 64
RET_V_DIM = 128
D_SB = SB_HEADS * SB_HEAD_DIM
D_RET_QK = RET_HEADS * RET_QK_DIM
D_RET_V = RET_HEADS * RET_V_DIM
D_IN = 3 * D_SB + 2 * D_RET_QK + 2 * D_RET_V
D_FF = 2816
D_PLE = 256
ROPE_BASE = 10000.0
EPS = 1e-6

LANES = 128
VMEM_LIMIT = 56 * 1024 * 1024

PROJ_CHUNK = 512
FF_CHUNK = 256
SB_TQ = 256
SB_TK = 256
RET_CHUNK = 256

SB_Q_SCALE = SB_HEAD_DIM ** -0.5 * math.log2(math.e)


def _rms(x):
    return x * lax.rsqrt(jnp.mean(x * x, axis=-1, keepdims=True) + EPS)


SB_Q_COL = 0
SB_K_COL = SB_HEADS
SB_V_COL = SB_HEADS + SB_HEADS // 2
SB_COLS = 2 * SB_HEADS + SB_HEADS // 2


def _store_heads_expanded(dst_ref, col0, pairs):
    lane = lax.broadcasted_iota(jnp.int32, (1, LANES), 1)
    for h in range(SB_HEADS):
        blk = pairs[:, (h // 2) * LANES:(h // 2 + 1) * LANES]
        keep = (lane // SB_HEAD_DIM) == (h % 2)
        dst_ref[:, (col0 + h) * LANES:(col0 + h + 1) * LANES] = jnp.where(keep, blk, jnp.zeros_like(blk))


def _inproj_kernel(x_ref, g_ref, w_ref, sb_ref, kv_ref, ret_ref):
    xb = (_rms(x_ref[...]) * g_ref[...]).astype(BF16)
    for c in range(0, D_IN, PROJ_CHUNK):
        acc = jnp.dot(xb, w_ref[:, c:c + PROJ_CHUNK], preferred_element_type=F32)
        if c == 0:
            _store_heads_expanded(sb_ref, SB_Q_COL, (acc * SB_Q_SCALE).astype(BF16))
        elif c == D_SB:
            sb_ref[:, SB_K_COL * LANES:SB_V_COL * LANES] = acc.astype(BF16)
            kv_ref[:, :D_SB] = acc
        elif c == 2 * D_SB:
            _store_heads_expanded(sb_ref, SB_V_COL, acc.astype(BF16))
            kv_ref[:, D_SB:] = acc
        else:
            ret_ref[:, c - 3 * D_SB:c - 3 * D_SB + PROJ_CHUNK] = acc


def _inproj(x2, g, w_bf, tm):
    assert PROJ_CHUNK == D_SB
    n = x2.shape[0]
    return pl.pallas_call(
        _inproj_kernel,
        grid=(n // tm,),
        in_specs=[
            pl.BlockSpec((tm, D_MODEL), lambda i: (i, 0)),
            pl.BlockSpec((1, D_MODEL), lambda i: (0, 0)),
            pl.BlockSpec((D_MODEL, D_IN), lambda i: (0, 0)),
        ],
        out_specs=[
            pl.BlockSpec((tm, SB_COLS * LANES), lambda i: (i, 0)),
            pl.BlockSpec((tm, 2 * D_SB), lambda i: (i, 0)),
            pl.BlockSpec((tm, D_IN - 3 * D_SB), lambda i: (i, 0)),
        ],
        out_shape=[
            jax.ShapeDtypeStruct((n, SB_COLS * LANES), BF16),
            jax.ShapeDtypeStruct((n, 2 * D_SB), F32),
            jax.ShapeDtypeStruct((n, D_IN - 3 * D_SB), F32),
        ],
        compiler_params=pltpu.CompilerParams(
            dimension_semantics=("parallel",), vmem_limit_bytes=VMEM_LIMIT),
        name="inproj",
    )(x2, g, w_bf)


NEG_BIG = -1e30
SB_PIPE_DEPTH = 4
SB_SLOTS = 4


def _sb_kernel(q0_ref, q1_ref, k_ref, v0_ref, v1_ref, o_ref, u_buf, hl_buf, d_buf, a_buf, acc_ref,
               *, tq, tk, q_off, n_q, n_pairs):
    q_refs = (q0_ref, q1_ref)
    v_refs = (v0_ref, v1_ref)
    col_iota = lax.broadcasted_iota(jnp.int32, (1, tk), 1)
    row_iota = lax.broadcasted_iota(jnp.int32, (tq, 1), 0)
    jj = lax.broadcasted_iota(jnp.int32, (2 * tk, tk), 0)
    ss = lax.broadcasted_iota(jnp.int32, (2 * tk, tk), 1)
    tri2 = jnp.where((jj % tk) >= ss, 1.0, 0.0).astype(BF16)
    sign_bit = jnp.uint32(0x80000000)
    high_half = jnp.uint32(0xFFFF0000)

    def last_block(qi):
        return (q_off + (qi + 1) * tq + tk - 1) // tk - 1

    u_buf[...] = jnp.full(u_buf.shape, NEG_BIG, F32)
    d_buf[...] = jnp.full(d_buf.shape, NEG_BIG, F32)
    hl_buf[...] = jnp.zeros(hl_buf.shape, BF16)
    a_buf[...] = jnp.zeros(a_buf.shape, BF16)
    acc_ref[...] = jnp.zeros(acc_ref.shape, F32)

    def step(ws, r1, r2, st):
        (qi_m1, kb_m1), info_e1, info_m2, info_e2, info_m3, carries = st

        qi_m3, kb_m3, first_m3 = info_m3
        acc = jnp.where(first_m3, 0.0, acc_ref[...])
        for h in range(2):
            vh = v_refs[h][0, pl.ds(pl.multiple_of(kb_m3 * tk, tk), tk), :]
            acc = acc + jnp.dot(a_buf[r1, h], vh, preferred_element_type=F32)
        acc_ref[...] = acc
        o_ref[0, pl.ds(pl.multiple_of(qi_m3 * tq, tq), tq), :] = acc.astype(o_ref.dtype)

        for h in range(2):
            a_buf[ws, h] = jnp.exp2(d_buf[r1, h]).astype(BF16)

        new_carries = []
        for h in range(2):
            incl = jnp.dot(hl_buf[r1, h], tri2, preferred_element_type=F32)
            carry = jnp.where(info_m2[2], 0.0, carries[h])
            d_buf[ws, h] = u_buf[r2, h] - incl - carry
            new_carries.append(carry + incl[:, 0:1])

        for h in range(2):
            u = u_buf[r1, h]
            neg_abs = lax.bitcast_convert_type(lax.bitcast_convert_type(u, jnp.uint32) | sign_bit, F32)
            s = jnp.maximum(u, 0.0) + jnp.log2(1.0 + jnp.exp2(neg_abs))
            hi = lax.bitcast_convert_type(lax.bitcast_convert_type(s, jnp.uint32) & high_half, F32)
            hl_buf[ws, h, :, :tk] = hi.astype(BF16)
            hl_buf[ws, h, :, tk:] = (s - hi).astype(BF16)

        k = k_ref[0, pl.ds(pl.multiple_of(kb_m1 * tk, tk), tk), :]
        causal = (col_iota + kb_m1 * tk) < (row_iota + (q_off + qi_m1 * tq))
        for h in range(2):
            qh = q_refs[h][0, pl.ds(pl.multiple_of(qi_m1 * tq, tq), tq), :]
            u = lax.dot_general(qh, k, (((1,), (1,)), ((), ())), preferred_element_type=F32)
            u_buf[ws, h] = jnp.where(causal, u, NEG_BIG)

        first_m1 = kb_m1 == last_block(qi_m1)
        tile_done = kb_m1 == 0
        qi_n = jnp.where(tile_done, jnp.minimum(qi_m1 + 1, n_q - 1), qi_m1)
        kb_n = jnp.where(tile_done, jnp.where(qi_m1 + 1 < n_q, last_block(qi_n), 0), kb_m1 - 1)
        return (qi_n, kb_n), (qi_m1, kb_m1, first_m1), info_e1, info_m2, info_e2, tuple(new_carries)

    idle = (jnp.int32(0), jnp.int32(0), jnp.bool_(True))
    zcarry = jnp.zeros((tq, 1), F32)
    st = ((jnp.int32(0), jnp.int32(last_block(0))), idle, idle, idle, idle, (zcarry, zcarry))

    def trip(i, st):
        cur = 2 * (i & 1)
        st = step(cur, 3 - cur, 2 - cur, st)
        return step(cur + 1, cur, 3 - cur, st)

    n_steps = n_pairs + SB_PIPE_DEPTH
    lax.fori_loop(0, (n_steps + 1) // 2, trip, st)


def _sb_attention(q_arr, k_arr, v_arr, q_col, k_col, v_col, tq, tk, q_off):
    b, t_q, _ = q_arr.shape
    t_k = k_arr.shape[1]
    n_q = t_q // tq
    blocks = [(q_off + (qi + 1) * tq + tk - 1) // tk for qi in range(n_q)]
    assert t_k % tk == 0 and blocks[-1] <= t_k // tk and v_arr.shape[1] == t_k
    kern = functools.partial(_sb_kernel, tq=tq, tk=tk, q_off=q_off, n_q=n_q, n_pairs=sum(blocks))
    head_block = lambda t, col, hh: pl.BlockSpec((1, t, LANES), lambda bi, hp: (bi, 0, col + 2 * hp + hh))
    return pl.pallas_call(
        kern,
        grid=(b, D_SB // LANES),
        in_specs=[
            head_block(t_q, q_col, 0),
            head_block(t_q, q_col, 1),
            pl.BlockSpec((1, t_k, LANES), lambda bi, hp: (bi, 0, k_col + hp)),
            head_block(t_k, v_col, 0),
            head_block(t_k, v_col, 1),
        ],
        out_specs=pl.BlockSpec((1, t_q, LANES), lambda bi, hp: (bi, 0, hp)),
        out_shape=jax.ShapeDtypeStruct((b, t_q, D_SB), BF16),
        scratch_shapes=[
            pltpu.VMEM((SB_SLOTS, 2, tq, tk), F32),
            pltpu.VMEM((SB_SLOTS, 2, tq, 2 * tk), BF16),
            pltpu.VMEM((SB_SLOTS, 2, tq, tk), F32),
            pltpu.VMEM((SB_SLOTS, 2, tq, tk), BF16),
            pltpu.VMEM((tq, LANES), F32),
        ],
        compiler_params=pltpu.CompilerParams(
            dimension_semantics=("parallel", "parallel"), vmem_limit_bytes=VMEM_LIMIT),
        name="stick_breaking",
    )(q_arr, q_arr, k_arr, v_arr, v_arr)


def _ret_kernel(qk_ref, v_ref, g_ref, cos_ref, sin_ref, dmat_ref, qdec_ref, kdec_ref, cdec_ref,
                gn_ref, s0_ref, o_ref, sfin_ref, state_ref):
    c = pl.program_id(1)

    @pl.when(c == 0)
    def _():
        state_ref[...] = s0_ref[0]

    lane = lax.broadcasted_iota(jnp.int32, (1, LANES), 1)
    first_half = (lane % RET_QK_DIM) < (RET_QK_DIM // 2)
    head_masks = [(lane // RET_QK_DIM) == h for h in range(2)]
    cos = cos_ref[...]
    sin = sin_ref[...]

    def rotary(x):
        swapped = jnp.where(first_half,
                            pltpu.roll(x, LANES - RET_QK_DIM // 2, 1),
                            pltpu.roll(x, RET_QK_DIM // 2, 1))
        return x * cos + swapped * sin

    for p in range(RET_HEADS // 2):
        q = rotary(qk_ref[0, :, p * LANES:(p + 1) * LANES])
        k = rotary(qk_ref[0, :, D_RET_QK + p * LANES:D_RET_QK + (p + 1) * LANES]) * (RET_QK_DIM ** -0.5)
        q_b = q.astype(BF16)
        k_b = k.astype(BF16)
        q_dec = (q * qdec_ref[p]).astype(BF16)
        k_dec = (k * kdec_ref[p]).astype(BF16)
        rows = slice(p * LANES, (p + 1) * LANES)
        state = state_ref[rows, :]
        state_b = state.astype(BF16)
        new_state = state * cdec_ref[rows, :]
        for hh in range(2):
            h = 2 * p + hh
            cols = slice(h * RET_V_DIM, (h + 1) * RET_V_DIM)
            v_b = v_ref[0, :, cols].astype(BF16)
            qh = jnp.where(head_masks[hh], q_b, jnp.zeros_like(q_b))
            scores = lax.dot_general(qh, k_b, (((1,), (1,)), ((), ())),
                                     preferred_element_type=F32) * dmat_ref[h]
            intra = jnp.dot(scores.astype(BF16), v_b, preferred_element_type=F32)
            qdh = jnp.where(head_masks[hh], q_dec, jnp.zeros_like(q_dec))
            cross = jnp.dot(qdh, state_b, preferred_element_type=F32)
            kdh = jnp.where(head_masks[hh], k_dec, jnp.zeros_like(k_dec))
            new_state = new_state + lax.dot_general(kdh, v_b, (((0,), (0,)), ((), ())),
                                                    preferred_element_type=F32)
            o = _rms(intra + cross)
            gate = g_ref[0, :, cols]
            o_ref[0, :, cols] = (o * gn_ref[:, cols] * (gate * jax.nn.sigmoid(gate))).astype(o_ref.dtype)
        state_ref[rows, :] = new_state

    @pl.when(c == pl.num_programs(1) - 1)
    def _():
        sfin_ref[0] = state_ref[...]


def _ret_tables(chunk, pos0, t):
    half = RET_QK_DIM // 2
    inv = ROPE_BASE ** (-jnp.arange(half, dtype=F32) / half)
    ang = (pos0 + jnp.arange(t)).astype(F32)[:, None] * inv[None, :]
    cos, sin = jnp.cos(ang), jnp.sin(ang)
    cos_t = jnp.tile(cos, (1, LANES // half))
    sin_t = jnp.tile(jnp.concatenate([-sin, sin], axis=-1), (1, LANES // RET_QK_DIM))
    lg = jnp.log(1.0 - 2.0 ** (-5.0 - jnp.arange(RET_HEADS, dtype=F32)))
    idx = jnp.arange(chunk, dtype=F32)
    rel = idx[:, None] - idx[None, :]
    dmat = jnp.where(rel >= 0, jnp.exp(lg[:, None, None] * jnp.maximum(rel, 0.0)), 0.0)
    qdec = jnp.exp(lg[:, None] * (idx[None, :] + 1.0))
    kdec = jnp.exp(lg[:, None] * (chunk - 1.0 - idx[None, :]))
    expand = lambda a: jnp.repeat(a.reshape(RET_HEADS // 2, 2, chunk), RET_QK_DIM, axis=1).transpose(0, 2, 1)
    cdec = jnp.repeat(jnp.exp(lg * chunk), RET_QK_DIM)[:, None] * jnp.ones((1, RET_V_DIM), F32)
    return cos_t, sin_t, dmat, expand(qdec), expand(kdec), cdec


def _retention(ret_arr, ret_gn, state0, chunk, pos0):
    b, t, _ = ret_arr.shape
    n_c = t // chunk
    cos_t, sin_t, dmat, qdec, kdec, cdec = _ret_tables(chunk, pos0, t)
    const = lambda *shape: pl.BlockSpec(shape, lambda bi, ci: (0,) * len(shape))
    return pl.pallas_call(
        _ret_kernel,
        grid=(b, n_c),
        in_specs=[
            pl.BlockSpec((1, chunk, 2 * D_RET_QK), lambda bi, ci: (bi, ci, 0)),
            pl.BlockSpec((1, chunk, D_RET_V), lambda bi, ci: (bi, ci, 1)),
            pl.BlockSpec((1, chunk, D_RET_V), lambda bi, ci: (bi, ci, 2)),
            pl.BlockSpec((chunk, LANES), lambda bi, ci: (ci, 0)),
            pl.BlockSpec((chunk, LANES), lambda bi, ci: (ci, 0)),
            const(RET_HEADS, chunk, chunk),
            const(RET_HEADS // 2, chunk, LANES),
            const(RET_HEADS // 2, chunk, LANES),
            const(RET_HEADS * RET_QK_DIM, RET_V_DIM),
            const(1, D_RET_V),
            pl.BlockSpec((1, RET_HEADS * RET_QK_DIM, RET_V_DIM), lambda bi, ci: (bi, 0, 0)),
        ],
        out_specs=[
            pl.BlockSpec((1, chunk, D_RET_V), lambda bi, ci: (bi, ci, 0)),
            pl.BlockSpec((1, RET_HEADS * RET_QK_DIM, RET_V_DIM), lambda bi, ci: (bi, 0, 0)),
        ],
        out_shape=[
            jax.ShapeDtypeStruct((b, t, D_RET_V), BF16),
            jax.ShapeDtypeStruct((b, RET_HEADS * RET_QK_DIM, RET_V_DIM), F32),
        ],
        scratch_shapes=[pltpu.VMEM((RET_HEADS * RET_QK_DIM, RET_V_DIM), F32)],
        compiler_params=pltpu.CompilerParams(
            dimension_semantics=("parallel", "arbitrary"), vmem_limit_bytes=VMEM_LIMIT),
        name="retention",
    )(ret_arr, ret_arr, ret_arr, cos_t, sin_t, dmat, qdec, kdec, cdec, ret_gn, state0)


def _mlp_kernel(x_ref, osb_ref, oret_ref, p_ref, w_out_ref, g_ffn_ref, w_gate_ref, w_up_ref, w_down_ref,
                g_ple_ref, w_pg_ref, w_ple_ref, g_final_ref, y_ref):
    h = (x_ref[...]
         + jnp.dot(osb_ref[...], w_out_ref[:D_SB, :], preferred_element_type=F32)
         + jnp.dot(oret_ref[...], w_out_ref[D_SB:, :], preferred_element_type=F32))
    hn = (_rms(h) * g_ffn_ref[...]).astype(BF16)
    ffn = jnp.zeros_like(h)
    for c in range(0, D_FF, FF_CHUNK):
        gate = jnp.dot(hn, w_gate_ref[:, c:c + FF_CHUNK], preferred_element_type=F32)
        up = jnp.dot(hn, w_up_ref[:, c:c + FF_CHUNK], preferred_element_type=F32)
        act = (gate * jax.nn.sigmoid(gate) * up).astype(BF16)
        ffn = ffn + jnp.dot(act, w_down_ref[c:c + FF_CHUNK, :], preferred_element_type=F32)
    h = h + ffn
    hn = (_rms(h) * g_ple_ref[...]).astype(BF16)
    gate = jax.nn.sigmoid(jnp.dot(hn, w_pg_ref[...], preferred_element_type=F32))
    ple = jnp.dot(p_ref[...].astype(BF16), w_ple_ref[...], preferred_element_type=F32)
    h = h + ple * gate
    y_ref[...] = _rms(h) * g_final_ref[...]


def _mlp(x2, osb2, oret2, p2, w_out, g_ffn, w_gate, w_up, w_down, g_ple, w_pg, w_ple, g_final, tm):
    n = x2.shape[0]
    tok = lambda width: pl.BlockSpec((tm, width), lambda i: (i, 0))
    const = lambda a: pl.BlockSpec(a.shape, lambda i: (0, 0), pipeline_mode=pl.Buffered(1))
    weights = (w_out, g_ffn, w_gate, w_up, w_down, g_ple, w_pg, w_ple, g_final)
    return pl.pallas_call(
        _mlp_kernel,
        grid=(n // tm,),
        in_specs=[tok(D_MODEL), tok(D_SB), tok(D_RET_V), tok(D_PLE)] + [const(a) for a in weights],
        out_specs=tok(D_MODEL),
        out_shape=jax.ShapeDtypeStruct((n, D_MODEL), F32),
        compiler_params=pltpu.CompilerParams(
            dimension_semantics=("parallel",), vmem_limit_bytes=VMEM_LIMIT),
        name="token_mlp",
    )(x2, osb2, oret2, p2, *weights)


def _split_heads(a, b, t):
    return a.reshape(b, t, SB_HEADS, SB_HEAD_DIM).transpose(0, 2, 1, 3)


def _merge_heads_bf16(a):
    b, h, t, d = a.shape
    return a.transpose(0, 2, 1, 3).reshape(b, t, h * d).astype(BF16)


def _expand_heads(a):
    b, t, _ = a.shape
    keep = jnp.eye(2, dtype=a.dtype)[None, None, None, :, :, None]
    a = a.reshape(b, t, SB_HEADS // 2, 1, 2, SB_HEAD_DIM) * keep
    return a.reshape(b, t, 2 * D_SB)


def _stream(x, p, w, cache_k, cache_v, state0, tm, tq, chunk):
    b, t, _ = x.shape
    n = b * t
    x2 = x.reshape(n, D_MODEL)
    sb, kv, ret = _inproj(x2, w["g_mix"], w["w_in"], tm)
    sb3 = sb.reshape(b, t, SB_COLS * LANES)
    new_k = _split_heads(kv[:, :D_SB], b, t)
    new_v = _split_heads(kv[:, D_SB:], b, t)
    if cache_k is None:
        o_sb = _sb_attention(sb3, sb3, sb3, SB_Q_COL, SB_K_COL, SB_V_COL, tq, SB_TK, 0)
        pos0 = 0
    else:
        past = cache_k.shape[2]
        n_pad = -(past + t) % SB_TK
        k_new = sb3[:, :, SB_K_COL * LANES:SB_V_COL * LANES]
        v_new = sb3[:, :, SB_V_COL * LANES:]
        k_all = jnp.concatenate([_merge_heads_bf16(cache_k), k_new, jnp.zeros((b, n_pad, D_SB), BF16)], axis=1)
        v_all = jnp.concatenate([_expand_heads(_merge_heads_bf16(cache_v)), v_new,
                                 jnp.zeros((b, n_pad, 2 * D_SB), BF16)], axis=1)
        o_sb = _sb_attention(sb3, k_all, v_all, SB_Q_COL, 0, 0, tq, SB_TK, past)
        pos0 = past
    o_ret, new_state = _retention(ret.reshape(b, t, -1), w["ret_gn"], state0, chunk, pos0)
    y = _mlp(x2, o_sb.reshape(n, D_SB), o_ret.reshape(n, D_RET_V), p.reshape(n, D_PLE),
             w["w_out"], w["g_ffn"], w["w_ffn_gate"], w["w_ffn_up"], w["w_ffn_down"],
             w["g_ple"], w["w_ple_gate"], w["w_ple"], w["g_final"], tm)
    new_state = new_state.reshape(b, RET_HEADS, RET_QK_DIM, RET_V_DIM)
    return y.reshape(b, t, D_MODEL), new_k[None], new_v[None], new_state[None]


def kernel(x_prompt, x_sample, cache_sb_k, cache_sb_v, state_ret, p_prompt, p_sample, g_mix, w_in, ret_gn, w_out,
           g_ffn, w_ffn_gate, w_ffn_up, w_ffn_down, g_ple, w_ple_gate, w_ple, g_final):
    assert g_mix.shape[0] == 1, "single-layer model"
    w = {
        "g_mix": g_mix, "w_in": w_in[0].astype(BF16), "ret_gn": ret_gn,
        "w_out": w_out[0].astype(BF16), "g_ffn": g_ffn,
        "w_ffn_gate": w_ffn_gate[0].astype(BF16), "w_ffn_up": w_ffn_up[0].astype(BF16),
        "w_ffn_down": w_ffn_down[0].astype(BF16), "g_ple": g_ple,
        "w_ple_gate": w_ple_gate[0].astype(BF16), "w_ple": w_ple[0].astype(BF16),
        "g_final": g_final[None, :],
    }
    b_p = x_prompt.shape[0]
    b_s, t_s, _ = x_sample.shape
    zero_state = jnp.zeros((b_p, RET_HEADS * RET_QK_DIM, RET_V_DIM), F32)
    y_p, nk_p, nv_p, ns_p = _stream(x_prompt, p_prompt[0], w, None, None, zero_state,
                                    tm=256, tq=SB_TQ, chunk=RET_CHUNK)
    y_s, nk_s, nv_s, ns_s = _stream(x_sample, p_sample[0], w, cache_sb_k[0], cache_sb_v[0],
                                    state_ret[0].reshape(b_s, RET_HEADS * RET_QK_DIM, RET_V_DIM),
                                    tm=b_s * t_s, tq=t_s, chunk=t_s)
    return y_p, y_s, nk_p, nv_p, ns_p, nk_s, nv_s, ns_s
```

```python
import functools
import math

import jax
import jax.numpy as jnp
from jax import lax
from jax.experimental import pallas as pl
from jax.experimental.pallas import tpu as pltpu

F32 = jnp.float32
BF16 = jnp.bfloat16

D_MODEL = 1024
SB_HEADS = 8
SB_HEAD_DIM = 64
RET_HEADS = 4
RET_QK_DIM = 64
RET_V_DIM = 128
D_SB = SB_HEADS * SB_HEAD_DIM
D_RET_QK = RET_HEADS * RET_QK_DIM
D_RET_V = RET_HEADS * RET_V_DIM
D_IN = 3 * D_SB + 2 * D_RET_QK + 2 * D_RET_V
D_FF = 2816
D_PLE = 256
ROPE_BASE = 10000.0
EPS = 1e-6

LANES = 128
VMEM_LIMIT = 56 * 1024 * 1024

PROJ_CHUNK = 512
FF_CHUNK = 256
MLP_TM = 512
SB_TQ = 256
SB_TK = 256
RET_CHUNK = 256

SB_Q_SCALE = SB_HEAD_DIM ** -0.5 * math.log2(math.e)


def _rms(x):
    return x * lax.rsqrt(jnp.mean(x * x, axis=-1, keepdims=True) + EPS)


SB_Q_COL = 0
SB_K_COL = SB_HEADS
SB_V_COL = SB_HEADS + SB_HEADS // 2
SB_COLS = 2 * SB_HEADS + SB_HEADS // 2


def _store_heads_expanded(dst_ref, col0, pairs):
    lane = lax.broadcasted_iota(jnp.int32, (1, LANES), 1)
    for h in range(SB_HEADS):
        blk = pairs[:, (h // 2) * LANES:(h // 2 + 1) * LANES]
        keep = (lane // SB_HEAD_DIM) == (h % 2)
        dst_ref[:, (col0 + h) * LANES:(col0 + h + 1) * LANES] = jnp.where(keep, blk, jnp.zeros_like(blk))


def _store_heads_split(dst_ref, acc):
    nb, _, nt, _ = dst_ref.shape
    for h in range(SB_HEADS):
        piece = acc[:, h * SB_HEAD_DIM:(h + 1) * SB_HEAD_DIM]
        dst_ref[:, h, :, :] = piece.reshape(nb, nt, SB_HEAD_DIM)


def _inproj_kernel(x_ref, g_ref, w_ref, sb_ref, k_ref, v_ref, ret_ref):
    xb = (_rms(x_ref[...]) * g_ref[...]).astype(BF16)
    for c in range(0, D_IN, PROJ_CHUNK):
        acc = jnp.dot(xb, w_ref[:, c:c + PROJ_CHUNK], preferred_element_type=F32)
        if c == 0:
            _store_heads_expanded(sb_ref, SB_Q_COL, (acc * SB_Q_SCALE).astype(BF16))
        elif c == D_SB:
            sb_ref[:, SB_K_COL * LANES:SB_V_COL * LANES] = acc.astype(BF16)
            _store_heads_split(k_ref, acc)
        elif c == 2 * D_SB:
            _store_heads_expanded(sb_ref, SB_V_COL, acc.astype(BF16))
            _store_heads_split(v_ref, acc)
        else:
            ret_ref[:, c - 3 * D_SB:c - 3 * D_SB + PROJ_CHUNK] = acc


def _inproj(x2, g, w_bf, b, t, tm):
    assert PROJ_CHUNK == D_SB
    n = b * t
    if tm <= t:
        assert t % tm == 0
        head_block = pl.BlockSpec((1, SB_HEADS, tm, SB_HEAD_DIM), lambda i: (i // (t // tm), 0, i % (t // tm), 0))
    else:
        assert tm % t == 0
        head_block = pl.BlockSpec((tm // t, SB_HEADS, t, SB_HEAD_DIM), lambda i: (i, 0, 0, 0))
    head_shape = jax.ShapeDtypeStruct((b, SB_HEADS, t, SB_HEAD_DIM), F32)
    return pl.pallas_call(
        _inproj_kernel,
        grid=(n // tm,),
        in_specs=[
            pl.BlockSpec((tm, D_MODEL), lambda i: (i, 0)),
            pl.BlockSpec((1, D_MODEL), lambda i: (0, 0)),
            pl.BlockSpec((D_MODEL, D_IN), lambda i: (0, 0)),
        ],
        out_specs=[
            pl.BlockSpec((tm, SB_COLS * LANES), lambda i: (i, 0)),
            head_block,
            head_block,
            pl.BlockSpec((tm, D_IN - 3 * D_SB), lambda i: (i, 0)),
        ],
        out_shape=[
            jax.ShapeDtypeStruct((n, SB_COLS * LANES), BF16),
            head_shape,
            head_shape,
            jax.ShapeDtypeStruct((n, D_IN - 3 * D_SB), F32),
        ],
        compiler_params=pltpu.CompilerParams(
            dimension_semantics=("parallel",), vmem_limit_bytes=VMEM_LIMIT),
        name="inproj",
    )(x2, g, w_bf)


NEG_BIG = -1e30
SB_PIPE_DEPTH = 4
SB_SLOTS = 4


def _sb_kernel(q0_ref, q1_ref, k_ref, v0_ref, v1_ref, o_ref, u_buf, s_buf, d_buf, a_buf, acc_ref,
               *, tq, tk, q_off, n_q, n_pairs):
    q_refs = (q0_ref, q1_ref)
    v_refs = (v0_ref, v1_ref)
    col_iota = lax.broadcasted_iota(jnp.int32, (1, tk), 1)
    row_iota = lax.broadcasted_iota(jnp.int32, (tq, 1), 0)
    jj = lax.broadcasted_iota(jnp.int32, (tk, tk), 0)
    ss = lax.broadcasted_iota(jnp.int32, (tk, tk), 1)
    tri = jnp.where(jj >= ss, 1.0, 0.0).astype(BF16)
    sign_bit = jnp.uint32(0x80000000)

    def last_block(qi):
        return (q_off + (qi + 1) * tq + tk - 1) // tk - 1

    u_buf[...] = jnp.full(u_buf.shape, NEG_BIG, F32)
    d_buf[...] = jnp.full(d_buf.shape, NEG_BIG, F32)
    s_buf[...] = jnp.zeros(s_buf.shape, BF16)
    a_buf[...] = jnp.zeros(a_buf.shape, BF16)
    acc_ref[...] = jnp.zeros(acc_ref.shape, F32)

    def step(ws, r1, r2, st):
        (qi_m1, kb_m1), info_e1, info_m2, info_e2, info_m3, carries = st

        qi_m3, kb_m3, first_m3 = info_m3
        acc = jnp.where(first_m3, 0.0, acc_ref[...])
        for h in range(2):
            vh = v_refs[h][0, pl.ds(pl.multiple_of(kb_m3 * tk, tk), tk), :]
            acc = acc + jnp.dot(a_buf[r1, h], vh, preferred_element_type=F32)
        acc_ref[...] = acc
        o_ref[0, pl.ds(pl.multiple_of(qi_m3 * tq, tq), tq), :] = acc.astype(o_ref.dtype)

        for h in range(2):
            a_buf[ws, h] = jnp.exp2(d_buf[r1, h]).astype(BF16)

        new_carries = []
        for h in range(2):
            incl = jnp.dot(s_buf[r1, h], tri, preferred_element_type=F32)
            carry = jnp.where(info_m2[2], 0.0, carries[h])
            d_buf[ws, h] = u_buf[r2, h] - incl - carry
            new_carries.append(carry + incl[:, 0:1])

        for h in range(2):
            u = u_buf[r1, h]
            neg_abs = lax.bitcast_convert_type(lax.bitcast_convert_type(u, jnp.uint32) | sign_bit, F32)
            s = jnp.maximum(u, 0.0) + jnp.log2(1.0 + jnp.exp2(neg_abs))
            s_buf[ws, h] = s.astype(BF16)

        k = k_ref[0, pl.ds(pl.multiple_of(kb_m1 * tk, tk), tk), :]
        causal = (col_iota + kb_m1 * tk) < (row_iota + (q_off + qi_m1 * tq))
        for h in range(2):
            qh = q_refs[h][0, pl.ds(pl.multiple_of(qi_m1 * tq, tq), tq), :]
            u = lax.dot_general(qh, k, (((1,), (1,)), ((), ())), preferred_element_type=F32)
            u_buf[ws, h] = jnp.where(causal, u, NEG_BIG)

        first_m1 = kb_m1 == last_block(qi_m1)
        tile_done = kb_m1 == 0
        qi_n = jnp.where(tile_done, jnp.minimum(qi_m1 + 1, n_q - 1), qi_m1)
        kb_n = jnp.where(tile_done, jnp.where(qi_m1 + 1 < n_q, last_block(qi_n), 0), kb_m1 - 1)
        return (qi_n, kb_n), (qi_m1, kb_m1, first_m1), info_e1, info_m2, info_e2, tuple(new_carries)

    idle = (jnp.int32(0), jnp.int32(0), jnp.bool_(True))
    zcarry = jnp.zeros((tq, 1), F32)
    st = ((jnp.int32(0), jnp.int32(last_block(0))), idle, idle, idle, idle, (zcarry, zcarry))

    def trip(i, st):
        cur = 2 * (i & 1)
        st = step(cur, 3 - cur, 2 - cur, st)
        return step(cur + 1, cur, 3 - cur, st)

    n_steps = n_pairs + SB_PIPE_DEPTH
    lax.fori_loop(0, (n_steps + 1) // 2, trip, st)


def _sb_attention(q_arr, k_arr, v_arr, q_col, k_col, v_col, tq, tk, q_off):
    b, t_q, _ = q_arr.shape
    t_k = k_arr.shape[1]
    n_q = t_q // tq
    blocks = [(q_off + (qi + 1) * tq + tk - 1) // tk for qi in range(n_q)]
    assert t_k % tk == 0 and blocks[-1] <= t_k // tk and v_arr.shape[1] == t_k
    kern = functools.partial(_sb_kernel, tq=tq, tk=tk, q_off=q_off, n_q=n_q, n_pairs=sum(blocks))
    head_block = lambda t, col, hh: pl.BlockSpec((1, t, LANES), lambda bi, hp: (bi, 0, col + 2 * hp + hh))
    return pl.pallas_call(
        kern,
        grid=(b, D_SB // LANES),
        in_specs=[
            head_block(t_q, q_col, 0),
            head_block(t_q, q_col, 1),
            pl.BlockSpec((1, t_k, LANES), lambda bi, hp: (bi, 0, k_col + hp)),
            head_block(t_k, v_col, 0),
            head_block(t_k, v_col, 1),
        ],
        out_specs=pl.BlockSpec((1, t_q, LANES), lambda bi, hp: (bi, 0, hp)),
        out_shape=jax.ShapeDtypeStruct((b, t_q, D_SB), BF16),
        scratch_shapes=[
            pltpu.VMEM((SB_SLOTS, 2, tq, tk), F32),
            pltpu.VMEM((SB_SLOTS, 2, tq, tk), BF16),
            pltpu.VMEM((SB_SLOTS, 2, tq, tk), F32),
            pltpu.VMEM((SB_SLOTS, 2, tq, tk), BF16),
            pltpu.VMEM((tq, LANES), F32),
        ],
        compiler_params=pltpu.CompilerParams(
            dimension_semantics=("parallel", "parallel"), vmem_limit_bytes=VMEM_LIMIT),
        name="stick_breaking",
    )(q_arr, q_arr, k_arr, v_arr, v_arr)


def _ret_kernel(qk_ref, v_ref, g_ref, cos_ref, sin_ref, dmat_ref, qdec_ref, kdec_ref, cdec_ref,
                gn_ref, s0_ref, o_ref, sfin_ref, state_ref):
    c = pl.program_id(1)

    @pl.when(c == 0)
    def _():
        state_ref[...] = s0_ref[0]

    lane = lax.broadcasted_iota(jnp.int32, (1, LANES), 1)
    first_half = (lane % RET_QK_DIM) < (RET_QK_DIM // 2)
    head_masks = [(lane // RET_QK_DIM) == h for h in range(2)]
    cos = cos_ref[...]
    sin = sin_ref[...]

    def rotary(x):
        swapped = jnp.where(first_half,
                            pltpu.roll(x, LANES - RET_QK_DIM // 2, 1),
                            pltpu.roll(x, RET_QK_DIM // 2, 1))
        return x * cos + swapped * sin

    for p in range(RET_HEADS // 2):
        q = rotary(qk_ref[0, :, p * LANES:(p + 1) * LANES])
        k = rotary(qk_ref[0, :, D_RET_QK + p * LANES:D_RET_QK + (p + 1) * LANES]) * (RET_QK_DIM ** -0.5)
        q_b = q.astype(BF16)
        k_b = k.astype(BF16)
        q_dec = (q * qdec_ref[p]).astype(BF16)
        k_dec = (k * kdec_ref[p]).astype(BF16)
        rows = slice(p * LANES, (p + 1) * LANES)
        state = state_ref[rows, :]
        state_b = state.astype(BF16)
        new_state = state * cdec_ref[rows, :]
        for hh in range(2):
            h = 2 * p + hh
            cols = slice(h * RET_V_DIM, (h + 1) * RET_V_DIM)
            v_b = v_ref[0, :, cols].astype(BF16)
            qh = jnp.where(head_masks[hh], q_b, jnp.zeros_like(q_b))
            scores = lax.dot_general(qh, k_b, (((1,), (1,)), ((), ())),
                                     preferred_element_type=F32) * dmat_ref[h]
            intra = jnp.dot(scores.astype(BF16), v_b, preferred_element_type=F32)
            qdh = jnp.where(head_masks[hh], q_dec, jnp.zeros_like(q_dec))
            cross = jnp.dot(qdh, state_b, preferred_element_type=F32)
            kdh = jnp.where(head_masks[hh], k_dec, jnp.zeros_like(k_dec))
            new_state = new_state + lax.dot_general(kdh, v_b, (((0,), (0,)), ((), ())),
                                                    preferred_element_type=F32)
            o = _rms(intra + cross)
            gate = g_ref[0, :, cols]
            o_ref[0, :, cols] = (o * gn_ref[:, cols] * (gate * jax.nn.sigmoid(gate))).astype(o_ref.dtype)
        state_ref[rows, :] = new_state

    @pl.when(c == pl.num_programs(1) - 1)
    def _():
        sfin_ref[0] = state_ref[...]


def _ret_tables(chunk, pos0, t):
    half = RET_QK_DIM // 2
    inv = ROPE_BASE ** (-jnp.arange(half, dtype=F32) / half)
    ang = (pos0 + jnp.arange(t)).astype(F32)[:, None] * inv[None, :]
    cos, sin = jnp.cos(ang), jnp.sin(ang)
    cos_t = jnp.tile(cos, (1, LANES // half))
    sin_t = jnp.tile(jnp.concatenate([-sin, sin], axis=-1), (1, LANES // RET_QK_DIM))
    lg = jnp.log(1.0 - 2.0 ** (-5.0 - jnp.arange(RET_HEADS, dtype=F32)))
    idx = jnp.arange(chunk, dtype=F32)
    rel = idx[:, None] - idx[None, :]
    dmat = jnp.where(rel >= 0, jnp.exp(lg[:, None, None] * jnp.maximum(rel, 0.0)), 0.0)
    qdec = jnp.exp(lg[:, None] * (idx[None, :] + 1.0))
    kdec = jnp.exp(lg[:, None] * (chunk - 1.0 - idx[None, :]))
    expand = lambda a: jnp.repeat(a.reshape(RET_HEADS // 2, 2, chunk), RET_QK_DIM, axis=1).transpose(0, 2, 1)
    cdec = jnp.repeat(jnp.exp(lg * chunk), RET_QK_DIM)[:, None] * jnp.ones((1, RET_V_DIM), F32)
    return cos_t, sin_t, dmat, expand(qdec), expand(kdec), cdec


def _retention(ret_arr, ret_gn, state0, chunk, pos0):
    b, t, _ = ret_arr.shape
    n_c = t // chunk
    cos_t, sin_t, dmat, qdec, kdec, cdec = _ret_tables(chunk, pos0, t)
    const = lambda *shape: pl.BlockSpec(shape, lambda bi, ci: (0,) * len(shape))
    return pl.pallas_call(
        _ret_kernel,
        grid=(b, n_c),
        in_specs=[
            pl.BlockSpec((1, chunk, 2 * D_RET_QK), lambda bi, ci: (bi, ci, 0)),
            pl.BlockSpec((1, chunk, D_RET_V), lambda bi, ci: (bi, ci, 1)),
            pl.BlockSpec((1, chunk, D_RET_V), lambda bi, ci: (bi, ci, 2)),
            pl.BlockSpec((chunk, LANES), lambda bi, ci: (ci, 0)),
            pl.BlockSpec((chunk, LANES), lambda bi, ci: (ci, 0)),
            const(RET_HEADS, chunk, chunk),
            const(RET_HEADS // 2, chunk, LANES),
            const(RET_HEADS // 2, chunk, LANES),
            const(RET_HEADS * RET_QK_DIM, RET_V_DIM),
            const(1, D_RET_V),
            pl.BlockSpec((1, RET_HEADS * RET_QK_DIM, RET_V_DIM), lambda bi, ci: (bi, 0, 0)),
        ],
        out_specs=[
            pl.BlockSpec((1, chunk, D_RET_V), lambda bi, ci: (bi, ci, 0)),
            pl.BlockSpec((1, RET_HEADS * RET_QK_DIM, RET_V_DIM), lambda bi, ci: (bi, 0, 0)),
        ],
        out_shape=[
            jax.ShapeDtypeStruct((b, t, D_RET_V), BF16),
            jax.ShapeDtypeStruct((b, RET_HEADS * RET_QK_DIM, RET_V_DIM), F32),
        ],
        scratch_shapes=[pltpu.VMEM((RET_HEADS * RET_QK_DIM, RET_V_DIM), F32)],
        compiler_params=pltpu.CompilerParams(
            dimension_semantics=("parallel", "arbitrary"), vmem_limit_bytes=VMEM_LIMIT),
        name="retention",
    )(ret_arr, ret_arr, ret_arr, cos_t, sin_t, dmat, qdec, kdec, cdec, ret_gn, state0)


def _mlp_kernel(x_ref, osb_ref, oret_ref, p_ref, w_out_ref, g_ffn_ref, w_gate_ref, w_up_ref, w_down_ref,
                g_ple_ref, w_pg_ref, w_ple_ref, g_final_ref, y_ref):
    h = (x_ref[...]
         + jnp.dot(osb_ref[...], w_out_ref[:D_SB, :], preferred_element_type=F32)
         + jnp.dot(oret_ref[...], w_out_ref[D_SB:, :], preferred_element_type=F32))
    hn = (_rms(h) * g_ffn_ref[...]).astype(BF16)
    ffn = jnp.zeros_like(h)
    for c in range(0, D_FF, FF_CHUNK):
        gate = jnp.dot(hn, w_gate_ref[:, c:c + FF_CHUNK], preferred_element_type=F32)
        up = jnp.dot(hn, w_up_ref[:, c:c + FF_CHUNK], preferred_element_type=F32)
        act = (gate * jax.nn.sigmoid(gate) * up).astype(BF16)
        ffn = ffn + jnp.dot(act, w_down_ref[c:c + FF_CHUNK, :], preferred_element_type=F32)
    h = h + ffn
    hn = (_rms(h) * g_ple_ref[...]).astype(BF16)
    gate = jax.nn.sigmoid(jnp.dot(hn, w_pg_ref[...], preferred_element_type=F32))
    ple = jnp.dot(p_ref[...].astype(BF16), w_ple_ref[...], preferred_element_type=F32)
    h = h + ple * gate
    y_ref[...] = _rms(h) * g_final_ref[...]


def _mlp(x2, osb2, oret2, p2, w_out, g_ffn, w_gate, w_up, w_down, g_ple, w_pg, w_ple, g_final, tm):
    n = x2.shape[0]
    tok = lambda width: pl.BlockSpec((tm, width), lambda i: (i, 0))
    const = lambda a: pl.BlockSpec(a.shape, lambda i: (0, 0), pipeline_mode=pl.Buffered(1))
    weights = (w_out, g_ffn, w_gate, w_up, w_down, g_ple, w_pg, w_ple, g_final)
    return pl.pallas_call(
        _mlp_kernel,
        grid=(n // tm,),
        in_specs=[tok(D_MODEL), tok(D_SB), tok(D_RET_V), tok(D_PLE)] + [const(a) for a in weights],
        out_specs=tok(D_MODEL),
        out_shape=jax.ShapeDtypeStruct((n, D_MODEL), F32),
        compiler_params=pltpu.CompilerParams(
            dimension_semantics=("parallel",), vmem_limit_bytes=VMEM_LIMIT),
        name="token_mlp",
    )(x2, osb2, oret2, p2, *weights)


def _merge_heads_bf16(a):
    b, h, t, d = a.shape
    return a.transpose(0, 2, 1, 3).reshape(b, t, h * d).astype(BF16)


def _expand_heads(a):
    b, t, _ = a.shape
    keep = jnp.eye(2, dtype=a.dtype)[None, None, None, :, :, None]
    a = a.reshape(b, t, SB_HEADS // 2, 1, 2, SB_HEAD_DIM) * keep
    return a.reshape(b, t, 2 * D_SB)


def _stream(x, p, w, cache_k, cache_v, state0, tm, tq, chunk):
    b, t, _ = x.shape
    n = b * t
    x2 = x.reshape(n, D_MODEL)
    sb, new_k, new_v, ret = _inproj(x2, w["g_mix"], w["w_in"], b, t, tm)
    sb3 = sb.reshape(b, t, SB_COLS * LANES)
    if cache_k is None:
        o_sb = _sb_attention(sb3, sb3, sb3, SB_Q_COL, SB_K_COL, SB_V_COL, tq, SB_TK, 0)
        pos0 = 0
    else:
        past = cache_k.shape[2]
        n_pad = -(past + t) % SB_TK
        k_new = sb3[:, :, SB_K_COL * LANES:SB_V_COL * LANES]
        v_new = sb3[:, :, SB_V_COL * LANES:]
        k_all = jnp.concatenate([_merge_heads_bf16(cache_k), k_new, jnp.zeros((b, n_pad, D_SB), BF16)], axis=1)
        v_all = jnp.concatenate([_expand_heads(_merge_heads_bf16(cache_v)), v_new,
                                 jnp.zeros((b, n_pad, 2 * D_SB), BF16)], axis=1)
        o_sb = _sb_attention(sb3, k_all, v_all, SB_Q_COL, 0, 0, tq, SB_TK, past)
        pos0 = past
    o_ret, new_state = _retention(ret.reshape(b, t, -1), w["ret_gn"], state0, chunk, pos0)
    y = _mlp(x2, o_sb.reshape(n, D_SB), o_ret.reshape(n, D_RET_V), p.reshape(n, D_PLE),
             w["w_out"], w["g_ffn"], w["w_ffn_gate"], w["w_ffn_up"], w["w_ffn_down"],
             w["g_ple"], w["w_ple_gate"], w["w_ple"], w["g_final"], min(n, MLP_TM))
    new_state = new_state.reshape(b, RET_HEADS, RET_QK_DIM, RET_V_DIM)
    return y.reshape(b, t, D_MODEL), new_k[None], new_v[None], new_state[None]


def kernel(x_prompt, x_sample, cache_sb_k, cache_sb_v, state_ret, p_prompt, p_sample, g_mix, w_in, ret_gn, w_out,
           g_ffn, w_ffn_gate, w_ffn_up, w_ffn_down, g_ple, w_ple_gate, w_ple, g_final):
    assert g_mix.shape[0] == 1, "single-layer model"
    w = {
        "g_mix": g_mix, "w_in": w_in[0].astype(BF16), "ret_gn": ret_gn,
        "w_out": w_out[0].astype(BF16), "g_ffn": g_ffn,
        "w_ffn_gate": w_ffn_gate[0].astype(BF16), "w_ffn_up": w_ffn_up[0].astype(BF16),
        "w_ffn_down": w_ffn_down[0].astype(BF16), "g_ple": g_ple,
        "w_ple_gate": w_ple_gate[0].astype(BF16), "w_ple": w_ple[0].astype(BF16),
        "g_final": g_final[None, :],
    }
    b_p = x_prompt.shape[0]
    b_s, t_s, _ = x_sample.shape
    zero_state = jnp.zeros((b_p, RET_HEADS * RET_QK_DIM, RET_V_DIM), F32)
    y_p, nk_p, nv_p, ns_p = _stream(x_prompt, p_prompt[0], w, None, None, zero_state,
                                    tm=512, tq=SB_TQ, chunk=RET_CHUNK)
    y_s, nk_s, nv_s, ns_s = _stream(x_sample, p_sample[0], w, cache_sb_k[0], cache_sb_v[0],
                                    state_ret[0].reshape(b_s, RET_HEADS * RET_QK_DIM, RET_V_DIM),
                                    tm=b_s * t_s, tq=t_s, chunk=t_s)
    return y_p, y_s, nk_p, nv_p, ns_p, nk_s, nv_s, ns_s
```

```python
import functools
import math

import jax
import jax.numpy as jnp
from jax import lax
from jax.experimental import pallas as pl
from jax.experimental.pallas import tpu as pltpu

F32 = jnp.float32
BF16 = jnp.bfloat16

D_MODEL = 1024
SB_HEADS = 8
SB_HEAD_DIM = 64
RET_HEADS = 4
RET_QK_DIM = 64
RET_V_DIM = 128
D_SB = SB_HEADS * SB_HEAD_DIM
D_RET_QK = RET_HEADS * RET_QK_DIM
D_RET_V = RET_HEADS * RET_V_DIM
D_IN = 3 * D_SB + 2 * D_RET_QK + 2 * D_RET_V
D_FF = 2816
D_PLE = 256
ROPE_BASE = 10000.0
EPS = 1e-6

LANES = 128
VMEM_LIMIT = 56 * 1024 * 1024

PROJ_CHUNK = 512
FF_CHUNK = 256
MLP_TM = 512
SB_TQ = 256
SB_TK = 256
RET_CHUNK = 256

SB_Q_SCALE = SB_HEAD_DIM ** -0.5 * math.log2(math.e)


def _rms(x):
    return x * lax.rsqrt(jnp.mean(x * x, axis=-1, keepdims=True) + EPS)


SB_Q_COL = 0
SB_K_COL = SB_HEADS
SB_V_COL = SB_HEADS + SB_HEADS // 2
SB_COLS = 2 * SB_HEADS + SB_HEADS // 2


def _store_heads_expanded(dst_ref, col0, pairs):
    lane = lax.broadcasted_iota(jnp.int32, (1, LANES), 1)
    for h in range(SB_HEADS):
        blk = pairs[:, (h // 2) * LANES:(h // 2 + 1) * LANES]
        keep = (lane // SB_HEAD_DIM) == (h % 2)
        dst_ref[:, (col0 + h) * LANES:(col0 + h + 1) * LANES] = jnp.where(keep, blk, jnp.zeros_like(blk))


def _store_heads_split(dst_ref, acc):
    nb, _, nt, _ = dst_ref.shape
    for h in range(SB_HEADS):
        piece = acc[:, h * SB_HEAD_DIM:(h + 1) * SB_HEAD_DIM]
        dst_ref[:, h, :, :] = piece.reshape(nb, nt, SB_HEAD_DIM)


def _inproj_kernel(x_ref, g_ref, w_ref, sb_ref, k_ref, v_ref, ret_ref):
    xb = (_rms(x_ref[...]) * g_ref[...]).astype(BF16)
    for c in range(0, D_IN, PROJ_CHUNK):
        acc = jnp.dot(xb, w_ref[:, c:c + PROJ_CHUNK], preferred_element_type=F32)
        if c == 0:
            _store_heads_expanded(sb_ref, SB_Q_COL, (acc * SB_Q_SCALE).astype(BF16))
        elif c == D_SB:
            sb_ref[:, SB_K_COL * LANES:SB_V_COL * LANES] = acc.astype(BF16)
            _store_heads_split(k_ref, acc)
        elif c == 2 * D_SB:
            _store_heads_expanded(sb_ref, SB_V_COL, acc.astype(BF16))
            _store_heads_split(v_ref, acc)
        else:
            ret_ref[:, c - 3 * D_SB:c - 3 * D_SB + PROJ_CHUNK] = acc


def _inproj(x2, g, w_bf, b, t, tm):
    assert PROJ_CHUNK == D_SB
    n = b * t
    if tm <= t:
        assert t % tm == 0
        head_block = pl.BlockSpec((1, SB_HEADS, tm, SB_HEAD_DIM), lambda i: (i // (t // tm), 0, i % (t // tm), 0))
    else:
        assert tm % t == 0
        head_block = pl.BlockSpec((tm // t, SB_HEADS, t, SB_HEAD_DIM), lambda i: (i, 0, 0, 0))
    head_shape = jax.ShapeDtypeStruct((b, SB_HEADS, t, SB_HEAD_DIM), F32)
    return pl.pallas_call(
        _inproj_kernel,
        grid=(n // tm,),
        in_specs=[
            pl.BlockSpec((tm, D_MODEL), lambda i: (i, 0)),
            pl.BlockSpec((1, D_MODEL), lambda i: (0, 0)),
            pl.BlockSpec((D_MODEL, D_IN), lambda i: (0, 0)),
        ],
        out_specs=[
            pl.BlockSpec((tm, SB_COLS * LANES), lambda i: (i, 0)),
            head_block,
            head_block,
            pl.BlockSpec((tm, D_IN - 3 * D_SB), lambda i: (i, 0)),
        ],
        out_shape=[
            jax.ShapeDtypeStruct((n, SB_COLS * LANES), BF16),
            head_shape,
            head_shape,
            jax.ShapeDtypeStruct((n, D_IN - 3 * D_SB), F32),
        ],
        compiler_params=pltpu.CompilerParams(
            dimension_semantics=("parallel",), vmem_limit_bytes=VMEM_LIMIT),
        name="inproj",
    )(x2, g, w_bf)


NEG_BIG = -1e30
SB_PIPE_DEPTH = 4
SB_UNROLL = 8
SB_SLOTS = 2 * SB_UNROLL


def _sb_kernel(q0_ref, q1_ref, k_ref, v0_ref, v1_ref, o_ref, u_buf, s_buf, d_buf, a_buf, acc_ref,
               *, tq, tk, q_off, n_q, n_pairs):
    q_refs = (q0_ref, q1_ref)
    v_refs = (v0_ref, v1_ref)
    col_iota = lax.broadcasted_iota(jnp.int32, (1, tk), 1)
    row_iota = lax.broadcasted_iota(jnp.int32, (tq, 1), 0)
    jj = lax.broadcasted_iota(jnp.int32, (tk, tk), 0)
    ss = lax.broadcasted_iota(jnp.int32, (tk, tk), 1)
    tri = jnp.where(jj >= ss, 1.0, 0.0).astype(BF16)
    sign_bit = jnp.uint32(0x80000000)

    def last_block(qi):
        return (q_off + (qi + 1) * tq + tk - 1) // tk - 1

    u_buf[...] = jnp.full(u_buf.shape, NEG_BIG, F32)
    d_buf[...] = jnp.full(d_buf.shape, NEG_BIG, F32)
    s_buf[...] = jnp.zeros(s_buf.shape, BF16)
    a_buf[...] = jnp.zeros(a_buf.shape, BF16)
    acc_ref[...] = jnp.zeros(acc_ref.shape, F32)

    def step(ws, r1, r2, st):
        (qi_m1, kb_m1), info_e1, info_m2, info_e2, info_m3, carries = st

        qi_m3, kb_m3, first_m3 = info_m3
        acc = jnp.where(first_m3, 0.0, acc_ref[...])
        for h in range(2):
            vh = v_refs[h][0, pl.ds(pl.multiple_of(kb_m3 * tk, tk), tk), :]
            acc = acc + jnp.dot(a_buf[r1, h], vh, preferred_element_type=F32)
        acc_ref[...] = acc
        o_ref[0, pl.ds(pl.multiple_of(qi_m3 * tq, tq), tq), :] = acc.astype(o_ref.dtype)

        for h in range(2):
            a_buf[ws, h] = jnp.exp2(d_buf[r1, h]).astype(BF16)

        new_carries = []
        for h in range(2):
            incl = jnp.dot(s_buf[r1, h], tri, preferred_element_type=F32)
            carry = jnp.where(info_m2[2], 0.0, carries[h])
            d_buf[ws, h] = u_buf[r2, h] - incl - carry
            new_carries.append(carry + incl[:, 0:1])

        for h in range(2):
            u = u_buf[r1, h]
            neg_abs = lax.bitcast_convert_type(lax.bitcast_convert_type(u, jnp.uint32) | sign_bit, F32)
            s = jnp.maximum(u, 0.0) + jnp.log2(1.0 + jnp.exp2(neg_abs))
            s_buf[ws, h] = s.astype(BF16)

        k = k_ref[0, pl.ds(pl.multiple_of(kb_m1 * tk, tk), tk), :]
        causal = (col_iota + kb_m1 * tk) < (row_iota + (q_off + qi_m1 * tq))
        for h in range(2):
            qh = q_refs[h][0, pl.ds(pl.multiple_of(qi_m1 * tq, tq), tq), :]
            u = lax.dot_general(qh, k, (((1,), (1,)), ((), ())), preferred_element_type=F32)
            u_buf[ws, h] = jnp.where(causal, u, NEG_BIG)

        first_m1 = kb_m1 == last_block(qi_m1)
        tile_done = kb_m1 == 0
        qi_n = jnp.where(tile_done, jnp.minimum(qi_m1 + 1, n_q - 1), qi_m1)
        kb_n = jnp.where(tile_done, jnp.where(qi_m1 + 1 < n_q, last_block(qi_n), 0), kb_m1 - 1)
        return (qi_n, kb_n), (qi_m1, kb_m1, first_m1), info_e1, info_m2, info_e2, tuple(new_carries)

    idle = (jnp.int32(0), jnp.int32(0), jnp.bool_(True))
    zcarry = jnp.zeros((tq, 1), F32)
    st = ((jnp.int32(0), jnp.int32(last_block(0))), idle, idle, idle, idle, (zcarry, zcarry))

    def trip(i, st):
        base = SB_UNROLL * (i & 1)
        other = SB_UNROLL - base
        for j in range(SB_UNROLL):
            back = lambda d: base + j - d if j >= d else other + SB_UNROLL + j - d
            st = step(base + j, back(1), back(2), st)
        return st

    n_steps = n_pairs + SB_PIPE_DEPTH
    lax.fori_loop(0, -(-n_steps // SB_UNROLL), trip, st)


def _sb_attention(q_arr, k_arr, v_arr, q_col, k_col, v_col, tq, tk, q_off):
    b, t_q, _ = q_arr.shape
    t_k = k_arr.shape[1]
    n_q = t_q // tq
    blocks = [(q_off + (qi + 1) * tq + tk - 1) // tk for qi in range(n_q)]
    assert t_k % tk == 0 and blocks[-1] <= t_k // tk and v_arr.shape[1] == t_k
    kern = functools.partial(_sb_kernel, tq=tq, tk=tk, q_off=q_off, n_q=n_q, n_pairs=sum(blocks))
    head_block = lambda t, col, hh: pl.BlockSpec((1, t, LANES), lambda bi, hp: (bi, 0, col + 2 * hp + hh))
    return pl.pallas_call(
        kern,
        grid=(b, D_SB // LANES),
        in_specs=[
            head_block(t_q, q_col, 0),
            head_block(t_q, q_col, 1),
            pl.BlockSpec((1, t_k, LANES), lambda bi, hp: (bi, 0, k_col + hp)),
            head_block(t_k, v_col, 0),
            head_block(t_k, v_col, 1),
        ],
        out_specs=pl.BlockSpec((1, t_q, LANES), lambda bi, hp: (bi, 0, hp)),
        out_shape=jax.ShapeDtypeStruct((b, t_q, D_SB), BF16),
        scratch_shapes=[
            pltpu.VMEM((SB_SLOTS, 2, tq, tk), F32),
            pltpu.VMEM((SB_SLOTS, 2, tq, tk), BF16),
            pltpu.VMEM((SB_SLOTS, 2, tq, tk), F32),
            pltpu.VMEM((SB_SLOTS, 2, tq, tk), BF16),
            pltpu.VMEM((tq, LANES), F32),
        ],
        compiler_params=pltpu.CompilerParams(
            dimension_semantics=("parallel", "parallel"), vmem_limit_bytes=VMEM_LIMIT),
        name="stick_breaking",
    )(q_arr, q_arr, k_arr, v_arr, v_arr)


def _ret_kernel(qk_ref, v_ref, g_ref, cos_ref, sin_ref, dmat_ref, qdec_ref, kdec_ref, cdec_ref,
                gn_ref, s0_ref, o_ref, sfin_ref, state_ref):
    c = pl.program_id(1)

    @pl.when(c == 0)
    def _():
        state_ref[...] = s0_ref[0]

    lane = lax.broadcasted_iota(jnp.int32, (1, LANES), 1)
    first_half = (lane % RET_QK_DIM) < (RET_QK_DIM // 2)
    head_masks = [(lane // RET_QK_DIM) == h for h in range(2)]
    cos = cos_ref[...]
    sin = sin_ref[...]

    def rotary(x):
        swapped = jnp.where(first_half,
                            pltpu.roll(x, LANES - RET_QK_DIM // 2, 1),
                            pltpu.roll(x, RET_QK_DIM // 2, 1))
        return x * cos + swapped * sin

    for p in range(RET_HEADS // 2):
        q = rotary(qk_ref[0, :, p * LANES:(p + 1) * LANES])
        k = rotary(qk_ref[0, :, D_RET_QK + p * LANES:D_RET_QK + (p + 1) * LANES]) * (RET_QK_DIM ** -0.5)
        q_b = q.astype(BF16)
        k_b = k.astype(BF16)
        q_dec = (q * qdec_ref[p]).astype(BF16)
        k_dec = (k * kdec_ref[p]).astype(BF16)
        rows = slice(p * LANES, (p + 1) * LANES)
        state = state_ref[rows, :]
        state_b = state.astype(BF16)
        new_state = state * cdec_ref[rows, :]
        for hh in range(2):
            h = 2 * p + hh
            cols = slice(h * RET_V_DIM, (h + 1) * RET_V_DIM)
            v_b = v_ref[0, :, cols].astype(BF16)
            qh = jnp.where(head_masks[hh], q_b, jnp.zeros_like(q_b))
            scores = lax.dot_general(qh, k_b, (((1,), (1,)), ((), ())),
                                     preferred_element_type=F32) * dmat_ref[h]
            intra = jnp.dot(scores.astype(BF16), v_b, preferred_element_type=F32)
            qdh = jnp.where(head_masks[hh], q_dec, jnp.zeros_like(q_dec))
            cross = jnp.dot(qdh, state_b, preferred_element_type=F32)
            kdh = jnp.where(head_masks[hh], k_dec, jnp.zeros_like(k_dec))
            new_state = new_state + lax.dot_general(kdh, v_b, (((0,), (0,)), ((), ())),
                                                    preferred_element_type=F32)
            o = _rms(intra + cross)
            gate = g_ref[0, :, cols]
            o_ref[0, :, cols] = (o * gn_ref[:, cols] * (gate * jax.nn.sigmoid(gate))).astype(o_ref.dtype)
        state_ref[rows, :] = new_state

    @pl.when(c == pl.num_programs(1) - 1)
    def _():
        sfin_ref[0] = state_ref[...]


def _ret_tables(chunk, pos0, t):
    half = RET_QK_DIM // 2
    inv = ROPE_BASE ** (-jnp.arange(half, dtype=F32) / half)
    ang = (pos0 + jnp.arange(t)).astype(F32)[:, None] * inv[None, :]
    cos, sin = jnp.cos(ang), jnp.sin(ang)
    cos_t = jnp.tile(cos, (1, LANES // half))
    sin_t = jnp.tile(jnp.concatenate([-sin, sin], axis=-1), (1, LANES // RET_QK_DIM))
    lg = jnp.log(1.0 - 2.0 ** (-5.0 - jnp.arange(RET_HEADS, dtype=F32)))
    idx = jnp.arange(chunk, dtype=F32)
    rel = idx[:, None] - idx[None, :]
    dmat = jnp.where(rel >= 0, jnp.exp(lg[:, None, None] * jnp.maximum(rel, 0.0)), 0.0)
    qdec = jnp.exp(lg[:, None] * (idx[None, :] + 1.0))
    kdec = jnp.exp(lg[:, None] * (chunk - 1.0 - idx[None, :]))
    expand = lambda a: jnp.repeat(a.reshape(RET_HEADS // 2, 2, chunk), RET_QK_DIM, axis=1).transpose(0, 2, 1)
    cdec = jnp.repeat(jnp.exp(lg * chunk), RET_QK_DIM)[:, None] * jnp.ones((1, RET_V_DIM), F32)
    return cos_t, sin_t, dmat, expand(qdec), expand(kdec), cdec


def _retention(ret_arr, ret_gn, state0, chunk, pos0):
    b, t, _ = ret_arr.shape
    n_c = t // chunk
    cos_t, sin_t, dmat, qdec, kdec, cdec = _ret_tables(chunk, pos0, t)
    const = lambda *shape: pl.BlockSpec(shape, lambda bi, ci: (0,) * len(shape))
    return pl.pallas_call(
        _ret_kernel,
        grid=(b, n_c),
        in_specs=[
            pl.BlockSpec((1, chunk, 2 * D_RET_QK), lambda bi, ci: (bi, ci, 0)),
            pl.BlockSpec((1, chunk, D_RET_V), lambda bi, ci: (bi, ci, 1)),
            pl.BlockSpec((1, chunk, D_RET_V), lambda bi, ci: (bi, ci, 2)),
            pl.BlockSpec((chunk, LANES), lambda bi, ci: (ci, 0)),
            pl.BlockSpec((chunk, LANES), lambda bi, ci: (ci, 0)),
            const(RET_HEADS, chunk, chunk),
            const(RET_HEADS // 2, chunk, LANES),
            const(RET_HEADS // 2, chunk, LANES),
            const(RET_HEADS * RET_QK_DIM, RET_V_DIM),
            const(1, D_RET_V),
            pl.BlockSpec((1, RET_HEADS * RET_QK_DIM, RET_V_DIM), lambda bi, ci: (bi, 0, 0)),
        ],
        out_specs=[
            pl.BlockSpec((1, chunk, D_RET_V), lambda bi, ci: (bi, ci, 0)),
            pl.BlockSpec((1, RET_HEADS * RET_QK_DIM, RET_V_DIM), lambda bi, ci: (bi, 0, 0)),
        ],
        out_shape=[
            jax.ShapeDtypeStruct((b, t, D_RET_V), BF16),
            jax.ShapeDtypeStruct((b, RET_HEADS * RET_QK_DIM, RET_V_DIM), F32),
        ],
        scratch_shapes=[pltpu.VMEM((RET_HEADS * RET_QK_DIM, RET_V_DIM), F32)],
        compiler_params=pltpu.CompilerParams(
            dimension_semantics=("parallel", "arbitrary"), vmem_limit_bytes=VMEM_LIMIT),
        name="retention",
    )(ret_arr, ret_arr, ret_arr, cos_t, sin_t, dmat, qdec, kdec, cdec, ret_gn, state0)


def _mlp_kernel(x_ref, osb_ref, oret_ref, p_ref, w_out_ref, g_ffn_ref, w_gate_ref, w_up_ref, w_down_ref,
                g_ple_ref, w_pg_ref, w_ple_ref, g_final_ref, y_ref):
    h = (x_ref[...]
         + jnp.dot(osb_ref[...], w_out_ref[:D_SB, :], preferred_element_type=F32)
         + jnp.dot(oret_ref[...], w_out_ref[D_SB:, :], preferred_element_type=F32))
    hn = (_rms(h) * g_ffn_ref[...]).astype(BF16)
    ffn = jnp.zeros_like(h)
    for c in range(0, D_FF, FF_CHUNK):
        gate = jnp.dot(hn, w_gate_ref[:, c:c + FF_CHUNK], preferred_element_type=F32)
        up = jnp.dot(hn, w_up_ref[:, c:c + FF_CHUNK], preferred_element_type=F32)
        act = (gate * jax.nn.sigmoid(gate) * up).astype(BF16)
        ffn = ffn + jnp.dot(act, w_down_ref[c:c + FF_CHUNK, :], preferred_element_type=F32)
    h = h + ffn
    hn = (_rms(h) * g_ple_ref[...]).astype(BF16)
    gate = jax.nn.sigmoid(jnp.dot(hn, w_pg_ref[...], preferred_element_type=F32))
    ple = jnp.dot(p_ref[...].astype(BF16), w_ple_ref[...], preferred_element_type=F32)
    h = h + ple * gate
    y_ref[...] = _rms(h) * g_final_ref[...]


def _mlp(x2, osb2, oret2, p2, w_out, g_ffn, w_gate, w_up, w_down, g_ple, w_pg, w_ple, g_final, tm):
    n = x2.shape[0]
    tok = lambda width: pl.BlockSpec((tm, width), lambda i: (i, 0))
    const = lambda a: pl.BlockSpec(a.shape, lambda i: (0, 0), pipeline_mode=pl.Buffered(1))
    weights = (w_out, g_ffn, w_gate, w_up, w_down, g_ple, w_pg, w_ple, g_final)
    return pl.pallas_call(
        _mlp_kernel,
        grid=(n // tm,),
        in_specs=[tok(D_MODEL), tok(D_SB), tok(D_RET_V), tok(D_PLE)] + [const(a) for a in weights],
        out_specs=tok(D_MODEL),
        out_shape=jax.ShapeDtypeStruct((n, D_MODEL), F32),
        compiler_params=pltpu.CompilerParams(
            dimension_semantics=("parallel",), vmem_limit_bytes=VMEM_LIMIT),
        name="token_mlp",
    )(x2, osb2, oret2, p2, *weights)


def _merge_heads_bf16(a):
    b, h, t, d = a.shape
    return a.transpose(0, 2, 1, 3).reshape(b, t, h * d).astype(BF16)


def _expand_heads(a):
    b, t, _ = a.shape
    keep = jnp.eye(2, dtype=a.dtype)[None, None, None, :, :, None]
    a = a.reshape(b, t, SB_HEADS // 2, 1, 2, SB_HEAD_DIM) * keep
    return a.reshape(b, t, 2 * D_SB)


def _stream(x, p, w, cache_k, cache_v, state0, tm, tq, chunk):
    b, t, _ = x.shape
    n = b * t
    x2 = x.reshape(n, D_MODEL)
    sb, new_k, new_v, ret = _inproj(x2, w["g_mix"], w["w_in"], b, t, tm)
    sb3 = sb.reshape(b, t, SB_COLS * LANES)
    if cache_k is None:
        o_sb = _sb_attention(sb3, sb3, sb3, SB_Q_COL, SB_K_COL, SB_V_COL, tq, SB_TK, 0)
        pos0 = 0
    else:
        past = cache_k.shape[2]
        n_pad = -(past + t) % SB_TK
        k_new = sb3[:, :, SB_K_COL * LANES:SB_V_COL * LANES]
        v_new = sb3[:, :, SB_V_COL * LANES:]
        k_all = jnp.concatenate([_merge_heads_bf16(cache_k), k_new, jnp.zeros((b, n_pad, D_SB), BF16)], axis=1)
        v_all = jnp.concatenate([_expand_heads(_merge_heads_bf16(cache_v)), v_new,
                                 jnp.zeros((b, n_pad, 2 * D_SB), BF16)], axis=1)
        o_sb = _sb_attention(sb3, k_all, v_all, SB_Q_COL, 0, 0, tq, SB_TK, past)
        pos0 = past
    o_ret, new_state = _retention(ret.reshape(b, t, -1), w["ret_gn"], state0, chunk, pos0)
    y = _mlp(x2, o_sb.reshape(n, D_SB), o_ret.reshape(n, D_RET_V), p.reshape(n, D_PLE),
             w["w_out"], w["g_ffn"], w["w_ffn_gate"], w["w_ffn_up"], w["w_ffn_down"],
             w["g_ple"], w["w_ple_gate"], w["w_ple"], w["g_final"], min(n, MLP_TM))
    new_state = new_state.reshape(b, RET_HEADS, RET_QK_DIM, RET_V_DIM)
    return y.reshape(b, t, D_MODEL), new_k[None], new_v[None], new_state[None]


def kernel(x_prompt, x_sample, cache_sb_k, cache_sb_v, state_ret, p_prompt, p_sample, g_mix, w_in, ret_gn, w_out,
           g_ffn, w_ffn_gate, w_ffn_up, w_ffn_down, g_ple, w_ple_gate, w_ple, g_final):
    assert g_mix.shape[0] == 1, "single-layer model"
    w = {
        "g_mix": g_mix, "w_in": w_in[0].astype(BF16), "ret_gn": ret_gn,
        "w_out": w_out[0].astype(BF16), "g_ffn": g_ffn,
        "w_ffn_gate": w_ffn_gate[0].astype(BF16), "w_ffn_up": w_ffn_up[0].astype(BF16),
        "w_ffn_down": w_ffn_down[0].astype(BF16), "g_ple": g_ple,
        "w_ple_gate": w_ple_gate[0].astype(BF16), "w_ple": w_ple[0].astype(BF16),
        "g_final": g_final[None, :],
    }
    b_p = x_prompt.shape[0]
    b_s, t_s, _ = x_sample.shape
    zero_state = jnp.zeros((b_p, RET_HEADS * RET_QK_DIM, RET_V_DIM), F32)
    y_p, nk_p, nv_p, ns_p = _stream(x_prompt, p_prompt[0], w, None, None, zero_state,
                                    tm=512, tq=SB_TQ, chunk=RET_CHUNK)
    y_s, nk_s, nv_s, ns_s = _stream(x_sample, p_sample[0], w, cache_sb_k[0], cache_sb_v[0],
                                    state_ret[0].reshape(b_s, RET_HEADS * RET_QK_DIM, RET_V_DIM),
                                    tm=b_s * t_s, tq=t_s, chunk=t_s)
    return y_p, y_s, nk_p, nv_p, ns_p, nk_s, nv_s, ns_s
```

```python
import functools
import math

import jax
import jax.numpy as jnp
from jax import lax
from jax.experimental import pallas as pl
from jax.experimental.pallas import tpu as pltpu

F32 = jnp.float32
BF16 = jnp.bfloat16

D_MODEL = 1024
SB_HEADS = 8
SB_HEAD_DIM = 64
RET_HEADS = 4
RET_QK_DIM = 64
RET_V_DIM = 128
D_SB = SB_HEADS * SB_HEAD_DIM
D_RET_QK = RET_HEADS * RET_QK_DIM
D_RET_V = RET_HEADS * RET_V_DIM
D_IN = 3 * D_SB + 2 * D_RET_QK + 2 * D_RET_V
D_FF = 2816
D_PLE = 256
ROPE_BASE = 10000.0
EPS = 1e-6

LANES = 128
VMEM_LIMIT = 56 * 1024 * 1024

PROJ_CHUNK = 512
FF_CHUNK = 256
MLP_TM = 512
SB_TQ = 256
SB_TK = 256
RET_CHUNK = 256

SB_Q_SCALE = SB_HEAD_DIM ** -0.5 * math.log2(math.e)


def _rms(x):
    return x * lax.rsqrt(jnp.mean(x * x, axis=-1, keepdims=True) + EPS)


SB_Q_COL = 0
SB_K_COL = SB_HEADS
SB_V_COL = SB_HEADS + SB_HEADS // 2
SB_COLS = 2 * SB_HEADS + SB_HEADS // 2


def _store_heads_expanded(dst_ref, col0, pairs):
    lane = lax.broadcasted_iota(jnp.int32, (1, LANES), 1)
    for h in range(SB_HEADS):
        blk = pairs[:, (h // 2) * LANES:(h // 2 + 1) * LANES]
        keep = (lane // SB_HEAD_DIM) == (h % 2)
        dst_ref[:, (col0 + h) * LANES:(col0 + h + 1) * LANES] = jnp.where(keep, blk, jnp.zeros_like(blk))


def _store_heads_split(dst_ref, acc):
    nb, _, nt, _ = dst_ref.shape
    for h in range(SB_HEADS):
        piece = acc[:, h * SB_HEAD_DIM:(h + 1) * SB_HEAD_DIM]
        dst_ref[:, h, :, :] = piece.reshape(nb, nt, SB_HEAD_DIM)


def _inproj_kernel(x_ref, g_ref, w_ref, sb_ref, k_ref, v_ref, ret_ref):
    xb = (_rms(x_ref[...]) * g_ref[...]).astype(BF16)
    for c in range(0, D_IN, PROJ_CHUNK):
        acc = jnp.dot(xb, w_ref[:, c:c + PROJ_CHUNK], preferred_element_type=F32)
        if c == 0:
            _store_heads_expanded(sb_ref, SB_Q_COL, (acc * SB_Q_SCALE).astype(BF16))
        elif c == D_SB:
            sb_ref[:, SB_K_COL * LANES:SB_V_COL * LANES] = acc.astype(BF16)
            _store_heads_split(k_ref, acc)
        elif c == 2 * D_SB:
            _store_heads_expanded(sb_ref, SB_V_COL, acc.astype(BF16))
            _store_heads_split(v_ref, acc)
        else:
            ret_ref[:, c - 3 * D_SB:c - 3 * D_SB + PROJ_CHUNK] = acc


def _inproj(x2, g, w_bf, b, t, tm):
    assert PROJ_CHUNK == D_SB
    n = b * t
    if tm <= t:
        assert t % tm == 0
        head_block = pl.BlockSpec((1, SB_HEADS, tm, SB_HEAD_DIM), lambda i: (i // (t // tm), 0, i % (t // tm), 0))
    else:
        assert tm % t == 0
        head_block = pl.BlockSpec((tm // t, SB_HEADS, t, SB_HEAD_DIM), lambda i: (i, 0, 0, 0))
    head_shape = jax.ShapeDtypeStruct((b, SB_HEADS, t, SB_HEAD_DIM), F32)
    return pl.pallas_call(
        _inproj_kernel,
        grid=(n // tm,),
        in_specs=[
            pl.BlockSpec((tm, D_MODEL), lambda i: (i, 0)),
            pl.BlockSpec((1, D_MODEL), lambda i: (0, 0)),
            pl.BlockSpec((D_MODEL, D_IN), lambda i: (0, 0)),
        ],
        out_specs=[
            pl.BlockSpec((tm, SB_COLS * LANES), lambda i: (i, 0)),
            head_block,
            head_block,
            pl.BlockSpec((tm, D_IN - 3 * D_SB), lambda i: (i, 0)),
        ],
        out_shape=[
            jax.ShapeDtypeStruct((n, SB_COLS * LANES), BF16),
            head_shape,
            head_shape,
            jax.ShapeDtypeStruct((n, D_IN - 3 * D_SB), F32),
        ],
        compiler_params=pltpu.CompilerParams(
            dimension_semantics=("parallel",), vmem_limit_bytes=VMEM_LIMIT),
        name="inproj",
    )(x2, g, w_bf)


NEG_BIG = -1e30
SB_PIPE_DEPTH = 3
SB_UNROLL = 8
SB_SLOTS = 2 * SB_UNROLL


def _sb_kernel(q0_ref, q1_ref, k_ref, v0_ref, v1_ref, o_ref, u_buf, s_buf, a_buf, acc_ref,
               *, tq, tk, q_off, n_q, n_pairs):
    q_refs = (q0_ref, q1_ref)
    v_refs = (v0_ref, v1_ref)
    col_iota = lax.broadcasted_iota(jnp.int32, (1, tk), 1)
    row_iota = lax.broadcasted_iota(jnp.int32, (tq, 1), 0)
    jj = lax.broadcasted_iota(jnp.int32, (tk, tk), 0)
    ss = lax.broadcasted_iota(jnp.int32, (tk, tk), 1)
    tri = jnp.where(jj >= ss, 1.0, 0.0).astype(BF16)
    sign_bit = jnp.uint32(0x80000000)

    def last_block(qi):
        return (q_off + (qi + 1) * tq + tk - 1) // tk - 1

    u_buf[...] = jnp.full(u_buf.shape, NEG_BIG, F32)
    s_buf[...] = jnp.zeros(s_buf.shape, BF16)
    a_buf[...] = jnp.zeros(a_buf.shape, BF16)
    acc_ref[...] = jnp.zeros(acc_ref.shape, F32)

    def step(ws, r1, r2, st):
        (qi_m1, kb_m1), info_e1, info_m2, info_m3, carries = st

        qi_m3, kb_m3, first_m3 = info_m3
        acc = jnp.where(first_m3, 0.0, acc_ref[...])
        for h in range(2):
            vh = v_refs[h][0, pl.ds(pl.multiple_of(kb_m3 * tk, tk), tk), :]
            acc = acc + jnp.dot(a_buf[r1, h], vh, preferred_element_type=F32)
        acc_ref[...] = acc
        o_ref[0, pl.ds(pl.multiple_of(qi_m3 * tq, tq), tq), :] = acc.astype(o_ref.dtype)

        new_carries = []
        for h in range(2):
            incl = jnp.dot(s_buf[r1, h], tri, preferred_element_type=F32)
            carry = jnp.where(info_m2[2], 0.0, carries[h])
            a_buf[ws, h] = jnp.exp2(u_buf[r2, h] - incl - carry).astype(BF16)
            new_carries.append(carry + incl[:, 0:1])

        for h in range(2):
            u = u_buf[r1, h]
            neg_abs = lax.bitcast_convert_type(lax.bitcast_convert_type(u, jnp.uint32) | sign_bit, F32)
            s = jnp.maximum(u, 0.0) + jnp.log2(1.0 + jnp.exp2(neg_abs))
            s_buf[ws, h] = s.astype(BF16)

        k = k_ref[0, pl.ds(pl.multiple_of(kb_m1 * tk, tk), tk), :]
        causal = (col_iota + kb_m1 * tk) < (row_iota + (q_off + qi_m1 * tq))
        for h in range(2):
            qh = q_refs[h][0, pl.ds(pl.multiple_of(qi_m1 * tq, tq), tq), :]
            u = lax.dot_general(qh, k, (((1,), (1,)), ((), ())), preferred_element_type=F32)
            u_buf[ws, h] = jnp.where(causal, u, NEG_BIG)

        first_m1 = kb_m1 == last_block(qi_m1)
        tile_done = kb_m1 == 0
        qi_n = jnp.where(tile_done, jnp.minimum(qi_m1 + 1, n_q - 1), qi_m1)
        kb_n = jnp.where(tile_done, jnp.where(qi_m1 + 1 < n_q, last_block(qi_n), 0), kb_m1 - 1)
        return (qi_n, kb_n), (qi_m1, kb_m1, first_m1), info_e1, info_m2, tuple(new_carries)

    idle = (jnp.int32(0), jnp.int32(0), jnp.bool_(True))
    zcarry = jnp.zeros((tq, 1), F32)
    st = ((jnp.int32(0), jnp.int32(last_block(0))), idle, idle, idle, (zcarry, zcarry))

    def trip(i, st):
        base = SB_UNROLL * (i & 1)
        other = SB_UNROLL - base
        for j in range(SB_UNROLL):
            back = lambda d: base + j - d if j >= d else other + SB_UNROLL + j - d
            st = step(base + j, back(1), back(2), st)
        return st

    n_steps = n_pairs + SB_PIPE_DEPTH
    lax.fori_loop(0, -(-n_steps // SB_UNROLL), trip, st)


def _sb_attention(q_arr, k_arr, v_arr, q_col, k_col, v_col, tq, tk, q_off):
    b, t_q, _ = q_arr.shape
    t_k = k_arr.shape[1]
    n_q = t_q // tq
    blocks = [(q_off + (qi + 1) * tq + tk - 1) // tk for qi in range(n_q)]
    assert t_k % tk == 0 and blocks[-1] <= t_k // tk and v_arr.shape[1] == t_k
    kern = functools.partial(_sb_kernel, tq=tq, tk=tk, q_off=q_off, n_q=n_q, n_pairs=sum(blocks))
    head_block = lambda t, col, hh: pl.BlockSpec((1, t, LANES), lambda bi, hp: (bi, 0, col + 2 * hp + hh))
    return pl.pallas_call(
        kern,
        grid=(b, D_SB // LANES),
        in_specs=[
            head_block(t_q, q_col, 0),
            head_block(t_q, q_col, 1),
            pl.BlockSpec((1, t_k, LANES), lambda bi, hp: (bi, 0, k_col + hp)),
            head_block(t_k, v_col, 0),
            head_block(t_k, v_col, 1),
        ],
        out_specs=pl.BlockSpec((1, t_q, LANES), lambda bi, hp: (bi, 0, hp)),
        out_shape=jax.ShapeDtypeStruct((b, t_q, D_SB), BF16),
        scratch_shapes=[
            pltpu.VMEM((SB_SLOTS, 2, tq, tk), F32),
            pltpu.VMEM((SB_SLOTS, 2, tq, tk), BF16),
            pltpu.VMEM((SB_SLOTS, 2, tq, tk), BF16),
            pltpu.VMEM((tq, LANES), F32),
        ],
        compiler_params=pltpu.CompilerParams(
            dimension_semantics=("parallel", "parallel"), vmem_limit_bytes=VMEM_LIMIT),
        name="stick_breaking",
    )(q_arr, q_arr, k_arr, v_arr, v_arr)


def _ret_kernel(qk_ref, v_ref, g_ref, cos_ref, sin_ref, dmat_ref, qdec_ref, kdec_ref, cdec_ref,
                gn_ref, s0_ref, o_ref, sfin_ref, state_ref):
    c = pl.program_id(1)

    @pl.when(c == 0)
    def _():
        state_ref[...] = s0_ref[0]

    lane = lax.broadcasted_iota(jnp.int32, (1, LANES), 1)
    first_half = (lane % RET_QK_DIM) < (RET_QK_DIM // 2)
    head_masks = [(lane // RET_QK_DIM) == h for h in range(2)]
    cos = cos_ref[...]
    sin = sin_ref[...]

    def rotary(x):
        swapped = jnp.where(first_half,
                            pltpu.roll(x, LANES - RET_QK_DIM // 2, 1),
                            pltpu.roll(x, RET_QK_DIM // 2, 1))
        return x * cos + swapped * sin

    for p in range(RET_HEADS // 2):
        q = rotary(qk_ref[0, :, p * LANES:(p + 1) * LANES])
        k = rotary(qk_ref[0, :, D_RET_QK + p * LANES:D_RET_QK + (p + 1) * LANES]) * (RET_QK_DIM ** -0.5)
        q_b = q.astype(BF16)
        k_b = k.astype(BF16)
        q_dec = (q * qdec_ref[p]).astype(BF16)
        k_dec = (k * kdec_ref[p]).astype(BF16)
        rows = slice(p * LANES, (p + 1) * LANES)
        state = state_ref[rows, :]
        state_b = state.astype(BF16)
        new_state = state * cdec_ref[rows, :]
        for hh in range(2):
            h = 2 * p + hh
            cols = slice(h * RET_V_DIM, (h + 1) * RET_V_DIM)
            v_b = v_ref[0, :, cols].astype(BF16)
            qh = jnp.where(head_masks[hh], q_b, jnp.zeros_like(q_b))
            scores = lax.dot_general(qh, k_b, (((1,), (1,)), ((), ())),
                                     preferred_element_type=F32) * dmat_ref[h]
            intra = jnp.dot(scores.astype(BF16), v_b, preferred_element_type=F32)
            qdh = jnp.where(head_masks[hh], q_dec, jnp.zeros_like(q_dec))
            cross = jnp.dot(qdh, state_b, preferred_element_type=F32)
            kdh = jnp.where(head_masks[hh], k_dec, jnp.zeros_like(k_dec))
            new_state = new_state + lax.dot_general(kdh, v_b, (((0,), (0,)), ((), ())),
                                                    preferred_element_type=F32)
            o = _rms(intra + cross)
            gate = g_ref[0, :, cols]
            o_ref[0, :, cols] = (o * gn_ref[:, cols] * (gate * jax.nn.sigmoid(gate))).astype(o_ref.dtype)
        state_ref[rows, :] = new_state

    @pl.when(c == pl.num_programs(1) - 1)
    def _():
        sfin_ref[0] = state_ref[...]


def _ret_tables(chunk, pos0, t):
    half = RET_QK_DIM // 2
    inv = ROPE_BASE ** (-jnp.arange(half, dtype=F32) / half)
    ang = (pos0 + jnp.arange(t)).astype(F32)[:, None] * inv[None, :]
    cos, sin = jnp.cos(ang), jnp.sin(ang)
    cos_t = jnp.tile(cos, (1, LANES // half))
    sin_t = jnp.tile(jnp.concatenate([-sin, sin], axis=-1), (1, LANES // RET_QK_DIM))
    lg = jnp.log(1.0 - 2.0 ** (-5.0 - jnp.arange(RET_HEADS, dtype=F32)))
    idx = jnp.arange(chunk, dtype=F32)
    rel = idx[:, None] - idx[None, :]
    dmat = jnp.where(rel >= 0, jnp.exp(lg[:, None, None] * jnp.maximum(rel, 0.0)), 0.0)
    qdec = jnp.exp(lg[:, None] * (idx[None, :] + 1.0))
    kdec = jnp.exp(lg[:, None] * (chunk - 1.0 - idx[None, :]))
    expand = lambda a: jnp.repeat(a.reshape(RET_HEADS // 2, 2, chunk), RET_QK_DIM, axis=1).transpose(0, 2, 1)
    cdec = jnp.repeat(jnp.exp(lg * chunk), RET_QK_DIM)[:, None] * jnp.ones((1, RET_V_DIM), F32)
    return cos_t, sin_t, dmat, expand(qdec), expand(kdec), cdec


def _retention(ret_arr, ret_gn, state0, chunk, pos0):
    b, t, _ = ret_arr.shape
    n_c = t // chunk
    cos_t, sin_t, dmat, qdec, kdec, cdec = _ret_tables(chunk, pos0, t)
    const = lambda *shape: pl.BlockSpec(shape, lambda bi, ci: (0,) * len(shape))
    return pl.pallas_call(
        _ret_kernel,
        grid=(b, n_c),
        in_specs=[
            pl.BlockSpec((1, chunk, 2 * D_RET_QK), lambda bi, ci: (bi, ci, 0)),
            pl.BlockSpec((1, chunk, D_RET_V), lambda bi, ci: (bi, ci, 1)),
            pl.BlockSpec((1, chunk, D_RET_V), lambda bi, ci: (bi, ci, 2)),
            pl.BlockSpec((chunk, LANES), lambda bi, ci: (ci, 0)),
            pl.BlockSpec((chunk, LANES), lambda bi, ci: (ci, 0)),
            const(RET_HEADS, chunk, chunk),
            const(RET_HEADS // 2, chunk, LANES),
            const(RET_HEADS // 2, chunk, LANES),
            const(RET_HEADS * RET_QK_DIM, RET_V_DIM),
            const(1, D_RET_V),
            pl.BlockSpec((1, RET_HEADS * RET_QK_DIM, RET_V_DIM), lambda bi, ci: (bi, 0, 0)),
        ],
        out_specs=[
            pl.BlockSpec((1, chunk, D_RET_V), lambda bi, ci: (bi, ci, 0)),
            pl.BlockSpec((1, RET_HEADS * RET_QK_DIM, RET_V_DIM), lambda bi, ci: (bi, 0, 0)),
        ],
        out_shape=[
            jax.ShapeDtypeStruct((b, t, D_RET_V), BF16),
            jax.ShapeDtypeStruct((b, RET_HEADS * RET_QK_DIM, RET_V_DIM), F32),
        ],
        scratch_shapes=[pltpu.VMEM((RET_HEADS * RET_QK_DIM, RET_V_DIM), F32)],
        compiler_params=pltpu.CompilerParams(
            dimension_semantics=("parallel", "arbitrary"), vmem_limit_bytes=VMEM_LIMIT),
        name="retention",
    )(ret_arr, ret_arr, ret_arr, cos_t, sin_t, dmat, qdec, kdec, cdec, ret_gn, state0)


def _mlp_kernel(x_ref, osb_ref, oret_ref, p_ref, w_out_ref, g_ffn_ref, w_gate_ref, w_up_ref, w_down_ref,
                g_ple_ref, w_pg_ref, w_ple_ref, g_final_ref, y_ref):
    h = (x_ref[...]
         + jnp.dot(osb_ref[...], w_out_ref[:D_SB, :], preferred_element_type=F32)
         + jnp.dot(oret_ref[...], w_out_ref[D_SB:, :], preferred_element_type=F32))
    hn = (_rms(h) * g_ffn_ref[...]).astype(BF16)
    ffn = jnp.zeros_like(h)
    for c in range(0, D_FF, FF_CHUNK):
        gate = jnp.dot(hn, w_gate_ref[:, c:c + FF_CHUNK], preferred_element_type=F32)
        up = jnp.dot(hn, w_up_ref[:, c:c + FF_CHUNK], preferred_element_type=F32)
        act = (gate * jax.nn.sigmoid(gate) * up).astype(BF16)
        ffn = ffn + jnp.dot(act, w_down_ref[c:c + FF_CHUNK, :], preferred_element_type=F32)
    h = h + ffn
    hn = (_rms(h) * g_ple_ref[...]).astype(BF16)
    gate = jax.nn.sigmoid(jnp.dot(hn, w_pg_ref[...], preferred_element_type=F32))
    ple = jnp.dot(p_ref[...].astype(BF16), w_ple_ref[...], preferred_element_type=F32)
    h = h + ple * gate
    y_ref[...] = _rms(h) * g_final_ref[...]


def _mlp(x2, osb2, oret2, p2, w_out, g_ffn, w_gate, w_up, w_down, g_ple, w_pg, w_ple, g_final, tm):
    n = x2.shape[0]
    tok = lambda width: pl.BlockSpec((tm, width), lambda i: (i, 0))
    const = lambda a: pl.BlockSpec(a.shape, lambda i: (0, 0), pipeline_mode=pl.Buffered(1))
    weights = (w_out, g_ffn, w_gate, w_up, w_down, g_ple, w_pg, w_ple, g_final)
    return pl.pallas_call(
        _mlp_kernel,
        grid=(n // tm,),
        in_specs=[tok(D_MODEL), tok(D_SB), tok(D_RET_V), tok(D_PLE)] + [const(a) for a in weights],
        out_specs=tok(D_MODEL),
        out_shape=jax.ShapeDtypeStruct((n, D_MODEL), F32),
        compiler_params=pltpu.CompilerParams(
            dimension_semantics=("parallel",), vmem_limit_bytes=VMEM_LIMIT),
        name="token_mlp",
    )(x2, osb2, oret2, p2, *weights)


def _merge_heads_bf16(a):
    b, h, t, d = a.shape
    return a.transpose(0, 2, 1, 3).reshape(b, t, h * d).astype(BF16)


def _expand_heads(a):
    b, t, _ = a.shape
    keep = jnp.eye(2, dtype=a.dtype)[None, None, None, :, :, None]
    a = a.reshape(b, t, SB_HEADS // 2, 1, 2, SB_HEAD_DIM) * keep
    return a.reshape(b, t, 2 * D_SB)


def _stream(x, p, w, cache_k, cache_v, state0, tm, tq, chunk):
    b, t, _ = x.shape
    n = b * t
    x2 = x.reshape(n, D_MODEL)
    sb, new_k, new_v, ret = _inproj(x2, w["g_mix"], w["w_in"], b, t, tm)
    sb3 = sb.reshape(b, t, SB_COLS * LANES)
    if cache_k is None:
        o_sb = _sb_attention(sb3, sb3, sb3, SB_Q_COL, SB_K_COL, SB_V_COL, tq, SB_TK, 0)
        pos0 = 0
    else:
        past = cache_k.shape[2]
        n_pad = -(past + t) % SB_TK
        k_new = sb3[:, :, SB_K_COL * LANES:SB_V_COL * LANES]
        v_new = sb3[:, :, SB_V_COL * LANES:]
        k_all = jnp.concatenate([_merge_heads_bf16(cache_k), k_new, jnp.zeros((b, n_pad, D_SB), BF16)], axis=1)
        v_all = jnp.concatenate([_expand_heads(_merge_heads_bf16(cache_v)), v_new,
                                 jnp.zeros((b, n_pad, 2 * D_SB), BF16)], axis=1)
        o_sb = _sb_attention(sb3, k_all, v_all, SB_Q_COL, 0, 0, tq, SB_TK, past)
        pos0 = past
    o_ret, new_state = _retention(ret.reshape(b, t, -1), w["ret_gn"], state0, chunk, pos0)
    y = _mlp(x2, o_sb.reshape(n, D_SB), o_ret.reshape(n, D_RET_V), p.reshape(n, D_PLE),
             w["w_out"], w["g_ffn"], w["w_ffn_gate"], w["w_ffn_up"], w["w_ffn_down"],
             w["g_ple"], w["w_ple_gate"], w["w_ple"], w["g_final"], min(n, MLP_TM))
    new_state = new_state.reshape(b, RET_HEADS, RET_QK_DIM, RET_V_DIM)
    return y.reshape(b, t, D_MODEL), new_k[None], new_v[None], new_state[None]


def kernel(x_prompt, x_sample, cache_sb_k, cache_sb_v, state_ret, p_prompt, p_sample, g_mix, w_in, ret_gn, w_out,
           g_ffn, w_ffn_gate, w_ffn_up, w_ffn_down, g_ple, w_ple_gate, w_ple, g_final):
    assert g_mix.shape[0] == 1, "single-layer model"
    w = {
        "g_mix": g_mix, "w_in": w_in[0].astype(BF16), "ret_gn": ret_gn,
        "w_out": w_out[0].astype(BF16), "g_ffn": g_ffn,
        "w_ffn_gate": w_ffn_gate[0].astype(BF16), "w_ffn_up": w_ffn_up[0].astype(BF16),
        "w_ffn_down": w_ffn_down[0].astype(BF16), "g_ple": g_ple,
        "w_ple_gate": w_ple_gate[0].astype(BF16), "w_ple": w_ple[0].astype(BF16),
        "g_final": g_final[None, :],
    }
    b_p = x_prompt.shape[0]
    b_s, t_s, _ = x_sample.shape
    zero_state = jnp.zeros((b_p, RET_HEADS * RET_QK_DIM, RET_V_DIM), F32)
    y_p, nk_p, nv_p, ns_p = _stream(x_prompt, p_prompt[0], w, None, None, zero_state,
                                    tm=512, tq=SB_TQ, chunk=RET_CHUNK)
    y_s, nk_s, nv_s, ns_s = _stream(x_sample, p_sample[0], w, cache_sb_k[0], cache_sb_v[0],
                                    state_ret[0].reshape(b_s, RET_HEADS * RET_QK_DIM, RET_V_DIM),
                                    tm=b_s * t_s, tq=t_s, chunk=t_s)
    return y_p, y_s, nk_p, nv_p, ns_p, nk_s, nv_s, ns_s
```

```python
import functools
import math

import jax
import jax.numpy as jnp
from jax import lax
from jax.experimental import pallas as pl
from jax.experimental.pallas import tpu as pltpu

F32 = jnp.float32
BF16 = jnp.bfloat16

D_MODEL = 1024
SB_HEADS = 8
SB_HEAD_DIM = 64
RET_HEADS = 4
RET_QK_DIM = 64
RET_V_DIM = 128
D_SB = SB_HEADS * SB_HEAD_DIM
D_RET_QK = RET_HEADS * RET_QK_DIM
D_RET_V = RET_HEADS * RET_V_DIM
D_IN = 3 * D_SB + 2 * D_RET_QK + 2 * D_RET_V
D_FF = 2816
D_PLE = 256
ROPE_BASE = 10000.0
EPS = 1e-6

LANES = 128
VMEM_LIMIT = 56 * 1024 * 1024

PROJ_CHUNK = 512
FF_CHUNK = 256
MLP_TM = 512
SB_TQ = 256
SB_TK = 256
RET_CHUNK = 256

SB_Q_SCALE = SB_HEAD_DIM ** -0.5 * math.log2(math.e)


def _rms(x):
    return x * lax.rsqrt(jnp.mean(x * x, axis=-1, keepdims=True) + EPS)


SB_Q_COL = 0
SB_K_COL = SB_HEADS
SB_V_COL = SB_HEADS + SB_HEADS // 2
SB_COLS = 2 * SB_HEADS + SB_HEADS // 2


def _store_heads_expanded(dst_ref, col0, pairs):
    lane = lax.broadcasted_iota(jnp.int32, (1, LANES), 1)
    for h in range(SB_HEADS):
        blk = pairs[:, (h // 2) * LANES:(h // 2 + 1) * LANES]
        keep = (lane // SB_HEAD_DIM) == (h % 2)
        dst_ref[:, (col0 + h) * LANES:(col0 + h + 1) * LANES] = jnp.where(keep, blk, jnp.zeros_like(blk))


def _store_heads_split(dst_ref, acc):
    nb, _, nt, _ = dst_ref.shape
    for h in range(SB_HEADS):
        piece = acc[:, h * SB_HEAD_DIM:(h + 1) * SB_HEAD_DIM]
        dst_ref[:, h, :, :] = piece.reshape(nb, nt, SB_HEAD_DIM)


def _inproj_kernel(x_ref, g_ref, w_ref, sb_ref, k_ref, v_ref, ret_ref):
    xb = (_rms(x_ref[...]) * g_ref[...]).astype(BF16)
    for c in range(0, D_IN, PROJ_CHUNK):
        acc = jnp.dot(xb, w_ref[:, c:c + PROJ_CHUNK], preferred_element_type=F32)
        if c == 0:
            _store_heads_expanded(sb_ref, SB_Q_COL, (acc * SB_Q_SCALE).astype(BF16))
        elif c == D_SB:
            sb_ref[:, SB_K_COL * LANES:SB_V_COL * LANES] = acc.astype(BF16)
            _store_heads_split(k_ref, acc)
        elif c == 2 * D_SB:
            _store_heads_expanded(sb_ref, SB_V_COL, acc.astype(BF16))
            _store_heads_split(v_ref, acc)
        else:
            ret_ref[:, c - 3 * D_SB:c - 3 * D_SB + PROJ_CHUNK] = acc


def _inproj(x2, g, w_bf, b, t, tm):
    assert PROJ_CHUNK == D_SB
    n = b * t
    if tm <= t:
        assert t % tm == 0
        head_block = pl.BlockSpec((1, SB_HEADS, tm, SB_HEAD_DIM), lambda i: (i // (t // tm), 0, i % (t // tm), 0))
    else:
        assert tm % t == 0
        head_block = pl.BlockSpec((tm // t, SB_HEADS, t, SB_HEAD_DIM), lambda i: (i, 0, 0, 0))
    head_shape = jax.ShapeDtypeStruct((b, SB_HEADS, t, SB_HEAD_DIM), F32)
    return pl.pallas_call(
        _inproj_kernel,
        grid=(n // tm,),
        in_specs=[
            pl.BlockSpec((tm, D_MODEL), lambda i: (i, 0)),
            pl.BlockSpec((1, D_MODEL), lambda i: (0, 0)),
            pl.BlockSpec((D_MODEL, D_IN), lambda i: (0, 0)),
        ],
        out_specs=[
            pl.BlockSpec((tm, SB_COLS * LANES), lambda i: (i, 0)),
            head_block,
            head_block,
            pl.BlockSpec((tm, D_IN - 3 * D_SB), lambda i: (i, 0)),
        ],
        out_shape=[
            jax.ShapeDtypeStruct((n, SB_COLS * LANES), BF16),
            head_shape,
            head_shape,
            jax.ShapeDtypeStruct((n, D_IN - 3 * D_SB), F32),
        ],
        compiler_params=pltpu.CompilerParams(
            dimension_semantics=("parallel",), vmem_limit_bytes=VMEM_LIMIT),
        name="inproj",
    )(x2, g, w_bf)


NEG_BIG = -1e30
SB_PIPE_DEPTH = 3
SB_UNROLL = 12
SB_SLOTS = 2 * SB_UNROLL


def _sb_kernel(q0_ref, q1_ref, k_ref, v0_ref, v1_ref, o_ref, u_buf, s_buf, a_buf, acc_ref,
               *, tq, tk, q_off, n_q, n_pairs):
    q_refs = (q0_ref, q1_ref)
    v_refs = (v0_ref, v1_ref)
    col_iota = lax.broadcasted_iota(jnp.int32, (1, tk), 1)
    row_iota = lax.broadcasted_iota(jnp.int32, (tq, 1), 0)
    jj = lax.broadcasted_iota(jnp.int32, (tk, tk), 0)
    ss = lax.broadcasted_iota(jnp.int32, (tk, tk), 1)
    tri = jnp.where(jj >= ss, 1.0, 0.0).astype(BF16)
    sign_bit = jnp.uint32(0x80000000)

    def last_block(qi):
        return (q_off + (qi + 1) * tq + tk - 1) // tk - 1

    u_buf[...] = jnp.full(u_buf.shape, NEG_BIG, F32)
    s_buf[...] = jnp.zeros(s_buf.shape, BF16)
    a_buf[...] = jnp.zeros(a_buf.shape, BF16)
    acc_ref[...] = jnp.zeros(acc_ref.shape, F32)

    def step(ws, r1, r2, st):
        (qi_m1, kb_m1), info_e1, info_m2, info_m3, carries = st

        qi_m3, kb_m3, first_m3 = info_m3
        acc = jnp.where(first_m3, 0.0, acc_ref[...])
        for h in range(2):
            vh = v_refs[h][0, pl.ds(pl.multiple_of(kb_m3 * tk, tk), tk), :]
            acc = acc + jnp.dot(a_buf[r1, h], vh, preferred_element_type=F32)
        acc_ref[...] = acc
        o_ref[0, pl.ds(pl.multiple_of(qi_m3 * tq, tq), tq), :] = acc.astype(o_ref.dtype)

        new_carries = []
        for h in range(2):
            incl = jnp.dot(s_buf[r1, h], tri, preferred_element_type=F32)
            carry = jnp.where(info_m2[2], 0.0, carries[h])
            a_buf[ws, h] = jnp.exp2(u_buf[r2, h] - incl - carry).astype(BF16)
            new_carries.append(carry + incl[:, 0:1])

        for h in range(2):
            u = u_buf[r1, h]
            neg_abs = lax.bitcast_convert_type(lax.bitcast_convert_type(u, jnp.uint32) | sign_bit, F32)
            s = jnp.maximum(u, 0.0) + jnp.log2(1.0 + jnp.exp2(neg_abs))
            s_buf[ws, h] = s.astype(BF16)

        k = k_ref[0, pl.ds(pl.multiple_of(kb_m1 * tk, tk), tk), :]
        causal = (col_iota + kb_m1 * tk) < (row_iota + (q_off + qi_m1 * tq))
        for h in range(2):
            qh = q_refs[h][0, pl.ds(pl.multiple_of(qi_m1 * tq, tq), tq), :]
            u = lax.dot_general(qh, k, (((1,), (1,)), ((), ())), preferred_element_type=F32)
            u_buf[ws, h] = jnp.where(causal, u, NEG_BIG)

        first_m1 = kb_m1 == last_block(qi_m1)
        tile_done = kb_m1 == 0
        qi_n = jnp.where(tile_done, jnp.minimum(qi_m1 + 1, n_q - 1), qi_m1)
        kb_n = jnp.where(tile_done, jnp.where(qi_m1 + 1 < n_q, last_block(qi_n), 0), kb_m1 - 1)
        return (qi_n, kb_n), (qi_m1, kb_m1, first_m1), info_e1, info_m2, tuple(new_carries)

    idle = (jnp.int32(0), jnp.int32(0), jnp.bool_(True))
    zcarry = jnp.zeros((tq, 1), F32)
    st = ((jnp.int32(0), jnp.int32(last_block(0))), idle, idle, idle, (zcarry, zcarry))

    def trip(i, st):
        base = SB_UNROLL * (i & 1)
        other = SB_UNROLL - base
        for j in range(SB_UNROLL):
            back = lambda d: base + j - d if j >= d else other + SB_UNROLL + j - d
            st = step(base + j, back(1), back(2), st)
        return st

    n_steps = n_pairs + SB_PIPE_DEPTH
    lax.fori_loop(0, -(-n_steps // SB_UNROLL), trip, st)


def _sb_attention(q_arr, k_arr, v_arr, q_col, k_col, v_col, tq, tk, q_off):
    b, t_q, _ = q_arr.shape
    t_k = k_arr.shape[1]
    n_q = t_q // tq
    blocks = [(q_off + (qi + 1) * tq + tk - 1) // tk for qi in range(n_q)]
    assert t_k % tk == 0 and blocks[-1] <= t_k // tk and v_arr.shape[1] == t_k
    kern = functools.partial(_sb_kernel, tq=tq, tk=tk, q_off=q_off, n_q=n_q, n_pairs=sum(blocks))
    head_block = lambda t, col, hh: pl.BlockSpec((1, t, LANES), lambda bi, hp: (bi, 0, col + 2 * hp + hh))
    return pl.pallas_call(
        kern,
        grid=(b, D_SB // LANES),
        in_specs=[
            head_block(t_q, q_col, 0),
            head_block(t_q, q_col, 1),
            pl.BlockSpec((1, t_k, LANES), lambda bi, hp: (bi, 0, k_col + hp)),
            head_block(t_k, v_col, 0),
            head_block(t_k, v_col, 1),
        ],
        out_specs=pl.BlockSpec((1, t_q, LANES), lambda bi, hp: (bi, 0, hp)),
        out_shape=jax.ShapeDtypeStruct((b, t_q, D_SB), BF16),
        scratch_shapes=[
            pltpu.VMEM((SB_SLOTS, 2, tq, tk), F32),
            pltpu.VMEM((SB_SLOTS, 2, tq, tk), BF16),
            pltpu.VMEM((SB_SLOTS, 2, tq, tk), BF16),
            pltpu.VMEM((tq, LANES), F32),
        ],
        compiler_params=pltpu.CompilerParams(
            dimension_semantics=("parallel", "parallel"), vmem_limit_bytes=VMEM_LIMIT),
        name="stick_breaking",
    )(q_arr, q_arr, k_arr, v_arr, v_arr)


def _ret_kernel(qk_ref, v_ref, g_ref, cos_ref, sin_ref, dmat_ref, qdec_ref, kdec_ref, cdec_ref,
                gn_ref, s0_ref, o_ref, sfin_ref, state_ref):
    c = pl.program_id(1)

    @pl.when(c == 0)
    def _():
        state_ref[...] = s0_ref[0]

    lane = lax.broadcasted_iota(jnp.int32, (1, LANES), 1)
    first_half = (lane % RET_QK_DIM) < (RET_QK_DIM // 2)
    head_masks = [(lane // RET_QK_DIM) == h for h in range(2)]
    cos = cos_ref[...]
    sin = sin_ref[...]

    def rotary(x):
        swapped = jnp.where(first_half,
                            pltpu.roll(x, LANES - RET_QK_DIM // 2, 1),
                            pltpu.roll(x, RET_QK_DIM // 2, 1))
        return x * cos + swapped * sin

    for p in range(RET_HEADS // 2):
        q = rotary(qk_ref[0, :, p * LANES:(p + 1) * LANES])
        k = rotary(qk_ref[0, :, D_RET_QK + p * LANES:D_RET_QK + (p + 1) * LANES]) * (RET_QK_DIM ** -0.5)
        q_b = q.astype(BF16)
        k_b = k.astype(BF16)
        q_dec = (q * qdec_ref[p]).astype(BF16)
        k_dec = (k * kdec_ref[p]).astype(BF16)
        rows = slice(p * LANES, (p + 1) * LANES)
        state = state_ref[rows, :]
        state_b = state.astype(BF16)
        new_state = state * cdec_ref[rows, :]
        for hh in range(2):
            h = 2 * p + hh
            cols = slice(h * RET_V_DIM, (h + 1) * RET_V_DIM)
            v_b = v_ref[0, :, cols].astype(BF16)
            qh = jnp.where(head_masks[hh], q_b, jnp.zeros_like(q_b))
            scores = lax.dot_general(qh, k_b, (((1,), (1,)), ((), ())),
                                     preferred_element_type=F32) * dmat_ref[h]
            intra = jnp.dot(scores.astype(BF16), v_b, preferred_element_type=F32)
            qdh = jnp.where(head_masks[hh], q_dec, jnp.zeros_like(q_dec))
            cross = jnp.dot(qdh, state_b, preferred_element_type=F32)
            kdh = jnp.where(head_masks[hh], k_dec, jnp.zeros_like(k_dec))
            new_state = new_state + lax.dot_general(kdh, v_b, (((0,), (0,)), ((), ())),
                                                    preferred_element_type=F32)
            o = _rms(intra + cross)
            gate = g_ref[0, :, cols]
            o_ref[0, :, cols] = (o * gn_ref[:, cols] * (gate * jax.nn.sigmoid(gate))).astype(o_ref.dtype)
        state_ref[rows, :] = new_state

    @pl.when(c == pl.num_programs(1) - 1)
    def _():
        sfin_ref[0] = state_ref[...]


def _ret_tables(chunk, pos0, t):
    half = RET_QK_DIM // 2
    inv = ROPE_BASE ** (-jnp.arange(half, dtype=F32) / half)
    ang = (pos0 + jnp.arange(t)).astype(F32)[:, None] * inv[None, :]
    cos, sin = jnp.cos(ang), jnp.sin(ang)
    cos_t = jnp.tile(cos, (1, LANES // half))
    sin_t = jnp.tile(jnp.concatenate([-sin, sin], axis=-1), (1, LANES // RET_QK_DIM))
    lg = jnp.log(1.0 - 2.0 ** (-5.0 - jnp.arange(RET_HEADS, dtype=F32)))
    idx = jnp.arange(chunk, dtype=F32)
    rel = idx[:, None] - idx[None, :]
    dmat = jnp.where(rel >= 0, jnp.exp(lg[:, None, None] * jnp.maximum(rel, 0.0)), 0.0)
    qdec = jnp.exp(lg[:, None] * (idx[None, :] + 1.0))
    kdec = jnp.exp(lg[:, None] * (chunk - 1.0 - idx[None, :]))
    expand = lambda a: jnp.repeat(a.reshape(RET_HEADS // 2, 2, chunk), RET_QK_DIM, axis=1).transpose(0, 2, 1)
    cdec = jnp.repeat(jnp.exp(lg * chunk), RET_QK_DIM)[:, None] * jnp.ones((1, RET_V_DIM), F32)
    return cos_t, sin_t, dmat, expand(qdec), expand(kdec), cdec


def _retention(ret_arr, ret_gn, state0, chunk, pos0):
    b, t, _ = ret_arr.shape
    n_c = t // chunk
    cos_t, sin_t, dmat, qdec, kdec, cdec = _ret_tables(chunk, pos0, t)
    const = lambda *shape: pl.BlockSpec(shape, lambda bi, ci: (0,) * len(shape))
    return pl.pallas_call(
        _ret_kernel,
        grid=(b, n_c),
        in_specs=[
            pl.BlockSpec((1, chunk, 2 * D_RET_QK), lambda bi, ci: (bi, ci, 0)),
            pl.BlockSpec((1, chunk, D_RET_V), lambda bi, ci: (bi, ci, 1)),
            pl.BlockSpec((1, chunk, D_RET_V), lambda bi, ci: (bi, ci, 2)),
            pl.BlockSpec((chunk, LANES), lambda bi, ci: (ci, 0)),
            pl.BlockSpec((chunk, LANES), lambda bi, ci: (ci, 0)),
            const(RET_HEADS, chunk, chunk),
            const(RET_HEADS // 2, chunk, LANES),
            const(RET_HEADS // 2, chunk, LANES),
            const(RET_HEADS * RET_QK_DIM, RET_V_DIM),
            const(1, D_RET_V),
            pl.BlockSpec((1, RET_HEADS * RET_QK_DIM, RET_V_DIM), lambda bi, ci: (bi, 0, 0)),
        ],
        out_specs=[
            pl.BlockSpec((1, chunk, D_RET_V), lambda bi, ci: (bi, ci, 0)),
            pl.BlockSpec((1, RET_HEADS * RET_QK_DIM, RET_V_DIM), lambda bi, ci: (bi, 0, 0)),
        ],
        out_shape=[
            jax.ShapeDtypeStruct((b, t, D_RET_V), BF16),
            jax.ShapeDtypeStruct((b, RET_HEADS * RET_QK_DIM, RET_V_DIM), F32),
        ],
        scratch_shapes=[pltpu.VMEM((RET_HEADS * RET_QK_DIM, RET_V_DIM), F32)],
        compiler_params=pltpu.CompilerParams(
            dimension_semantics=("parallel", "arbitrary"), vmem_limit_bytes=VMEM_LIMIT),
        name="retention",
    )(ret_arr, ret_arr, ret_arr, cos_t, sin_t, dmat, qdec, kdec, cdec, ret_gn, state0)


def _mlp_kernel(x_ref, osb_ref, oret_ref, p_ref, w_out_ref, g_ffn_ref, w_gate_ref, w_up_ref, w_down_ref,
                g_ple_ref, w_pg_ref, w_ple_ref, g_final_ref, y_ref):
    h = (x_ref[...]
         + jnp.dot(osb_ref[...], w_out_ref[:D_SB, :], preferred_element_type=F32)
         + jnp.dot(oret_ref[...], w_out_ref[D_SB:, :], preferred_element_type=F32))
    hn = (_rms(h) * g_ffn_ref[...]).astype(BF16)
    ffn = jnp.zeros_like(h)
    for c in range(0, D_FF, FF_CHUNK):
        gate = jnp.dot(hn, w_gate_ref[:, c:c + FF_CHUNK], preferred_element_type=F32)
        up = jnp.dot(hn, w_up_ref[:, c:c + FF_CHUNK], preferred_element_type=F32)
        act = (gate * jax.nn.sigmoid(gate) * up).astype(BF16)
        ffn = ffn + jnp.dot(act, w_down_ref[c:c + FF_CHUNK, :], preferred_element_type=F32)
    h = h + ffn
    hn = (_rms(h) * g_ple_ref[...]).astype(BF16)
    gate = jax.nn.sigmoid(jnp.dot(hn, w_pg_ref[...], preferred_element_type=F32))
    ple = jnp.dot(p_ref[...].astype(BF16), w_ple_ref[...], preferred_element_type=F32)
    h = h + ple * gate
    y_ref[...] = _rms(h) * g_final_ref[...]


def _mlp(x2, osb2, oret2, p2, w_out, g_ffn, w_gate, w_up, w_down, g_ple, w_pg, w_ple, g_final, tm):
    n = x2.shape[0]
    tok = lambda width: pl.BlockSpec((tm, width), lambda i: (i, 0))
    const = lambda a: pl.BlockSpec(a.shape, lambda i: (0, 0), pipeline_mode=pl.Buffered(1))
    weights = (w_out, g_ffn, w_gate, w_up, w_down, g_ple, w_pg, w_ple, g_final)
    return pl.pallas_call(
        _mlp_kernel,
        grid=(n // tm,),
        in_specs=[tok(D_MODEL), tok(D_SB), tok(D_RET_V), tok(D_PLE)] + [const(a) for a in weights],
        out_specs=tok(D_MODEL),
        out_shape=jax.ShapeDtypeStruct((n, D_MODEL), F32),
        compiler_params=pltpu.CompilerParams(
            dimension_semantics=("parallel",), vmem_limit_bytes=VMEM_LIMIT),
        name="token_mlp",
    )(x2, osb2, oret2, p2, *weights)


def _merge_heads_bf16(a):
    b, h, t, d = a.shape
    return a.transpose(0, 2, 1, 3).reshape(b, t, h * d).astype(BF16)


def _expand_heads(a):
    b, t, _ = a.shape
    keep = jnp.eye(2, dtype=a.dtype)[None, None, None, :, :, None]
    a = a.reshape(b, t, SB_HEADS // 2, 1, 2, SB_HEAD_DIM) * keep
    return a.reshape(b, t, 2 * D_SB)


def _stream(x, p, w, cache_k, cache_v, state0, tm, tq, chunk):
    b, t, _ = x.shape
    n = b * t
    x2 = x.reshape(n, D_MODEL)
    sb, new_k, new_v, ret = _inproj(x2, w["g_mix"], w["w_in"], b, t, tm)
    sb3 = sb.reshape(b, t, SB_COLS * LANES)
    if cache_k is None:
        o_sb = _sb_attention(sb3, sb3, sb3, SB_Q_COL, SB_K_COL, SB_V_COL, tq, SB_TK, 0)
        pos0 = 0
    else:
        past = cache_k.shape[2]
        n_pad = -(past + t) % SB_TK
        k_new = sb3[:, :, SB_K_COL * LANES:SB_V_COL * LANES]
        v_new = sb3[:, :, SB_V_COL * LANES:]
        k_all = jnp.concatenate([_merge_heads_bf16(cache_k), k_new, jnp.zeros((b, n_pad, D_SB), BF16)], axis=1)
        v_all = jnp.concatenate([_expand_heads(_merge_heads_bf16(cache_v)), v_new,
                                 jnp.zeros((b, n_pad, 2 * D_SB), BF16)], axis=1)
        o_sb = _sb_attention(sb3, k_all, v_all, SB_Q_COL, 0, 0, tq, SB_TK, past)
        pos0 = past
    o_ret, new_state = _retention(ret.reshape(b, t, -1), w["ret_gn"], state0, chunk, pos0)
    y = _mlp(x2, o_sb.reshape(n, D_SB), o_ret.reshape(n, D_RET_V), p.reshape(n, D_PLE),
             w["w_out"], w["g_ffn"], w["w_ffn_gate"], w["w_ffn_up"], w["w_ffn_down"],
             w["g_ple"], w["w_ple_gate"], w["w_ple"], w["g_final"], min(n, MLP_TM))
    new_state = new_state.reshape(b, RET_HEADS, RET_QK_DIM, RET_V_DIM)
    return y.reshape(b, t, D_MODEL), new_k[None], new_v[None], new_state[None]


def kernel(x_prompt, x_sample, cache_sb_k, cache_sb_v, state_ret, p_prompt, p_sample, g_mix, w_in, ret_gn, w_out,
           g_ffn, w_ffn_gate, w_ffn_up, w_ffn_down, g_ple, w_ple_gate, w_ple, g_final):
    assert g_mix.shape[0] == 1, "single-layer model"
    w = {
        "g_mix": g_mix, "w_in": w_in[0].astype(BF16), "ret_gn": ret_gn,
        "w_out": w_out[0].astype(BF16), "g_ffn": g_ffn,
        "w_ffn_gate": w_ffn_gate[0].astype(BF16), "w_ffn_up": w_ffn_up[0].astype(BF16),
        "w_ffn_down": w_ffn_down[0].astype(BF16), "g_ple": g_ple,
        "w_ple_gate": w_ple_gate[0].astype(BF16), "w_ple": w_ple[0].astype(BF16),
        "g_final": g_final[None, :],
    }
    b_p = x_prompt.shape[0]
    b_s, t_s, _ = x_sample.shape
    zero_state = jnp.zeros((b_p, RET_HEADS * RET_QK_DIM, RET_V_DIM), F32)
    y_p, nk_p, nv_p, ns_p = _stream(x_prompt, p_prompt[0], w, None, None, zero_state,
                                    tm=512, tq=SB_TQ, chunk=RET_CHUNK)
    y_s, nk_s, nv_s, ns_s = _stream(x_sample, p_sample[0], w, cache_sb_k[0], cache_sb_v[0],
                                    state_ret[0].reshape(b_s, RET_HEADS * RET_QK_DIM, RET_V_DIM),
                                    tm=b_s * t_s, tq=t_s, chunk=t_s)
    return y_p, y_s, nk_p, nv_p, ns_p, nk_s, nv_s, ns_s
```
